```python
import jax
import jax.numpy as jnp
from jax import lax
import numpy as np

D_MODEL = 2048
BATCH = 32
SEQ = 256
DEPTH = 4
DEC_BATCH = 4
DEC_SEQ = 2048
PAST_LEN = 512

GRID_W = 64
N_MIXERS = 3
N_POOL_LAYERS = (DEPTH + 2) // 3
N_SSD_LAYERS = (DEPTH + 1) // 3
N_CONV_LAYERS = DEPTH // 3
N_MOD = 9
D_FF = 5632
EPS = 1e-6
POOL_WINDOWS = (2, 4, 8, 16)
N_POOL_GROUPS = 4
POOL_GROUP_DIM = D_MODEL // N_POOL_GROUPS
SSD_D_INNER = 2 * D_MODEL
SSD_HEAD_DIM = 64
SSD_HEADS = SSD_D_INNER // SSD_HEAD_DIM
SSD_GROUPS = 8
SSD_STATE = 128
SSD_CONV = 5
SSD_CHUNK = 128
SSD_BC_DIM = SSD_GROUPS * SSD_STATE
SSD_CONV_DIM = SSD_D_INNER + 2 * SSD_BC_DIM
SSD_IN_DIM = SSD_D_INNER + SSD_CONV_DIM + 2 * SSD_HEADS
CM_KERNEL = 31

kernel_name = "hybrid_pool_ssd_conformer_diffusion_step"


def _rms(x, w):
    xf = x.astype(jnp.float32)
    y = xf * lax.rsqrt(jnp.mean(xf * xf, axis=-1, keepdims=True) + EPS)
    return (y * w.astype(jnp.float32)).astype(x.dtype)


def _layernorm(x, w, b):
    xf = x.astype(jnp.float32)
    mu = jnp.mean(xf, axis=-1, keepdims=True)
    var = jnp.mean(jnp.square(xf - mu), axis=-1, keepdims=True)
    y = (xf - mu) * lax.rsqrt(var + EPS)
    return (y * w.astype(jnp.float32) + b.astype(jnp.float32)).astype(x.dtype)


def _pre(x, norm_w, mod3):
    return _rms(x, norm_w) * (1 + mod3[:, 1][:, None]) + mod3[:, 0][:, None]


def _post_add(x, out, norm_w, gate, res_w):
    return x + res_w * gate[:, None] * _rms(out, norm_w)


def _swiglu(h, wg, wu, wd):
    return (jax.nn.silu(h @ wg) * (h @ wu)) @ wd


def _dwconv(x, w, bias):
    k = w.shape[0]
    out = lax.conv_general_dilated(
        x, w[:, None, :].astype(x.dtype), window_strides=(1,), padding=[(k // 2, k // 2)],
        dimension_numbers=("NWC", "WIO", "NWC"), feature_group_count=x.shape[-1])
    return out + bias.astype(x.dtype)


def _window_mean(v, axis, w):
    n = v.shape[axis]
    cs = jnp.cumsum(v, axis=axis)
    pad = [(0, 0)] * v.ndim
    pad[axis] = (1, 0)
    cs = jnp.pad(cs, pad)
    pos = jnp.arange(n)
    lo = jnp.clip(pos - w // 2, 0, n)
    hi = jnp.clip(pos + (w - w // 2), 0, n)
    s = jnp.take(cs, hi, axis=axis) - jnp.take(cs, lo, axis=axis)
    shape = [1] * v.ndim
    shape[axis] = n
    return s / (hi - lo).astype(v.dtype).reshape(shape)


def _pool_mixer(h, rows, pool_w, pool_scale):
    bsz, n, _ = h.shape
    hf = h.astype(jnp.float32).reshape(bsz, n, N_POOL_GROUPS, POOL_GROUP_DIM)
    outs = []
    for g, w in enumerate(POOL_WINDOWS):
        v = hf[:, :, g]
        if rows is None:
            m = _window_mean(v, 1, w)
        else:
            v2 = v.reshape(bsz, rows, GRID_W, POOL_GROUP_DIM)
            m = _window_mean(_window_mean(v2, 2, w), 1, w).reshape(bsz, n, POOL_GROUP_DIM)
        outs.append(m - v)
    pooled = jnp.stack(outs, axis=2).astype(h.dtype)
    mixed = jnp.einsum("blgc,gcd->blgd", pooled, pool_w).reshape(bsz, n, D_MODEL)
    return mixed * pool_scale


def _ssd_scan(xs, dt, a, bm, cm, h0):
    bsz, n, nh, hp = xs.shape
    nc = n // SSD_CHUNK
    r = nh // SSD_GROUPS
    x_dt = (xs * dt[..., None]).reshape(bsz, nc, SSD_CHUNK, SSD_GROUPS, r, hp)
    a_cum = jnp.cumsum((dt * a).reshape(bsz, nc, SSD_CHUNK, SSD_GROUPS, r), axis=2)
    b_c = bm.reshape(bsz, nc, SSD_CHUNK, SSD_GROUPS, SSD_STATE)
    c_c = cm.reshape(bsz, nc, SSD_CHUNK, SSD_GROUPS, SSD_STATE)
    lower = jnp.tril(jnp.ones((SSD_CHUNK, SSD_CHUNK), dtype=bool))[:, :, None, None]
    seg = a_cum[:, :, :, None] - a_cum[:, :, None, :]
    lmat = jnp.exp(jnp.where(lower, seg, -jnp.inf))
    cb = jnp.einsum("bclgn,bcsgn->bclsg", c_c, b_c)
    y_diag = jnp.einsum("bclsgr,bcsgrp->bclgrp", cb[..., None] * lmat, x_dt)
    decay_to_end = jnp.exp(a_cum[:, :, -1:] - a_cum)
    chunk_states = jnp.einsum("bclgn,bclgrp->bcgrpn", b_c, x_dt * decay_to_end[..., None])
    chunk_decay = jnp.exp(a_cum[:, :, -1])

    def step(carry, inp):
        st, dec = inp
        return carry * dec[..., None, None] + st, carry

    h_final, h_start = lax.scan(
        step, h0.reshape(bsz, SSD_GROUPS, r, hp, SSD_STATE),
        (jnp.moveaxis(chunk_states, 1, 0), jnp.moveaxis(chunk_decay, 1, 0)))
    h_start = jnp.moveaxis(h_start, 0, 1)
    y_off = jnp.einsum("bclgn,bcgrpn->bclgrp", c_c, h_start) * jnp.exp(a_cum)[..., None]
    y = (y_diag + y_off).reshape(bsz, n, nh, hp)
    return y, h_final.reshape(bsz, nh, hp, SSD_STATE)


def _flip(t):
    return jnp.flip(t, axis=1)


def _ssd_mixer(h, h0, in_w, conv_w, conv_b, a_log, dt_bias, d_skip, gnorm_w, out_w):
    bsz, n, _ = h.shape
    proj = h @ in_w
    z = proj[..., :SSD_D_INNER]
    xbc = proj[..., SSD_D_INNER:SSD_D_INNER + SSD_CONV_DIM]
    dt_raw = proj[..., SSD_D_INNER + SSD_CONV_DIM:]
    xbc = jax.nn.silu(_dwconv(xbc, conv_w, conv_b)).astype(jnp.float32)
    xs = xbc[..., :SSD_D_INNER].reshape(bsz, n, SSD_HEADS, SSD_HEAD_DIM)
    bm = xbc[..., SSD_D_INNER:SSD_D_INNER + SSD_BC_DIM].reshape(bsz, n, SSD_GROUPS, SSD_STATE)
    cm = xbc[..., SSD_D_INNER + SSD_BC_DIM:].reshape(bsz, n, SSD_GROUPS, SSD_STATE)
    dt = jax.nn.softplus(dt_raw.astype(jnp.float32).reshape(bsz, n, 2, SSD_HEADS)
                         + dt_bias.astype(jnp.float32))
    a = -jnp.exp(a_log.astype(jnp.float32))
    y_f, s_f = _ssd_scan(xs, dt[:, :, 0], a[0], bm, cm, h0[:, 0])
    y_b, s_b = _ssd_scan(_flip(xs), _flip(dt[:, :, 1]), a[1], _flip(bm), _flip(cm), h0[:, 1])
    y = y_f + _flip(y_b) + d_skip.astype(jnp.float32)[:, None] * xs
    y = y.reshape(bsz, n, SSD_D_INNER) * jax.nn.silu(z.astype(jnp.float32))
    y = _rms(y, gnorm_w).astype(h.dtype)
    return y @ out_w, jnp.stack([s_f, s_b], axis=1)


def _conv_module(h, pw1, dw_w, dw_b, ln_w, ln_b, pw2):
    u = h @ pw1
    u = u[..., :D_MODEL] * jax.nn.sigmoid(u[..., D_MODEL:])
    u = _dwconv(u, dw_w, dw_b)
    u = jax.nn.silu(_layernorm(u, ln_w, ln_b))
    return u @ pw2


def _trunk(x, cond, rows, h0_all, norm_w, ada_w, ada_b, ffn_wg, ffn_wu, ffn_wd,
           pool_w, pool_scale, ssd_in_w, ssd_conv_w, ssd_conv_b, ssd_a_log, ssd_dt_bias,
           ssd_d, ssd_norm_w, ssd_out_w, cm_pw1, cm_dw_w, cm_dw_b, cm_ln_w, cm_ln_b, cm_pw2):
    bsz = x.shape[0]
    sc = jax.nn.silu(cond)
    states = []
    for i in range(DEPTH):
        mod = (sc @ ada_w[i] + ada_b[i]).reshape(cond.shape[0], N_MOD, D_MODEL)
        h = _pre(x, norm_w[i, 0, 0], mod[:, 0:3])
        x = _post_add(x, _swiglu(h, ffn_wg[i, 0], ffn_wu[i, 0], ffn_wd[i, 0]),
                      norm_w[i, 0, 1], mod[:, 2], 0.5)
        kind, j = i % N_MIXERS, i // N_MIXERS
        h = _pre(x, norm_w[i, 1, 0], mod[:, 3:6])
        if kind == 0:
            out = _pool_mixer(h, rows, pool_w[j], pool_scale[j])
        elif kind == 1:
            if h0_all is None:
                h0 = jnp.zeros((bsz, 2, SSD_HEADS, SSD_HEAD_DIM, SSD_STATE), jnp.float32)
            else:
                h0 = h0_all[:, j].astype(jnp.float32)
            out, st = _ssd_mixer(h, h0, ssd_in_w[j], ssd_conv_w[j], ssd_conv_b[j], ssd_a_log[j],
                                 ssd_dt_bias[j], ssd_d[j], ssd_norm_w[j], ssd_out_w[j])
            states.append(st)
        else:
            out = _conv_module(h, cm_pw1[j], cm_dw_w[j], cm_dw_b[j], cm_ln_w[j], cm_ln_b[j], cm_pw2[j])
        x = _post_add(x, out, norm_w[i, 1, 1], mod[:, 5], 1.0)
        h = _pre(x, norm_w[i, 2, 0], mod[:, 6:9])
        x = _post_add(x, _swiglu(h, ffn_wg[i, 1], ffn_wu[i, 1], ffn_wd[i, 1]),
                      norm_w[i, 2, 1], mod[:, 8], 0.5)
    return x, states


def setup_inputs(seed: int = 0) -> dict:
    key = jax.random.key(seed)
    ks = jax.random.split(key, 32)
    f32 = jnp.float32

    def nrm(k, shape, scale):
        return jax.random.normal(k, shape, f32) * scale

    dt0 = jnp.exp(jax.random.uniform(ks[17], (N_SSD_LAYERS, 2, SSD_HEADS), f32,
                                     np.log(1e-3).astype(np.float32), np.log(1e-1).astype(np.float32)))
    return {
        "x_prompt": nrm(ks[0], (BATCH, SEQ, D_MODEL), 1.0),
        "x_sample": nrm(ks[1], (DEC_BATCH, DEC_SEQ, D_MODEL), 1.0),
        "state_ssd": nrm(ks[2], (DEC_BATCH, N_SSD_LAYERS, 2, SSD_HEADS, SSD_HEAD_DIM, SSD_STATE), 0.1),
        "c": nrm(ks[3], (DEC_BATCH, D_MODEL), 1.0),
        "c_ctx": nrm(ks[4], (D_MODEL,), 1.0),
        "norm_w": 1.0 + nrm(ks[5], (DEPTH, 3, 2, D_MODEL), 0.02),
        "ada_w": nrm(ks[6], (DEPTH, D_MODEL, N_MOD * D_MODEL), 0.5 * D_MODEL ** -0.5),
        "ada_b": nrm(ks[7], (DEPTH, N_MOD * D_MODEL), 0.02),
        "ffn_wg": nrm(ks[8], (DEPTH, 2, D_MODEL, D_FF), D_MODEL ** -0.5),
        "ffn_wu": nrm(ks[9], (DEPTH, 2, D_MODEL, D_FF), D_MODEL ** -0.5),
        "ffn_wd": nrm(ks[10], (DEPTH, 2, D_FF, D_MODEL), D_FF ** -0.5),
        "pool_w": nrm(ks[11], (N_POOL_LAYERS, N_POOL_GROUPS, POOL_GROUP_DIM, POOL_GROUP_DIM), POOL_GROUP_DIM ** -0.5),
        "pool_scale": 1.0 + nrm(ks[12], (N_POOL_LAYERS, D_MODEL), 0.02),
        "ssd_in_w": nrm(ks[13], (N_SSD_LAYERS, D_MODEL, SSD_IN_DIM), D_MODEL ** -0.5),
        "ssd_conv_w": nrm(ks[14], (N_SSD_LAYERS, SSD_CONV, SSD_CONV_DIM), SSD_CONV ** -0.5),
        "ssd_conv_b": nrm(ks[15], (N_SSD_LAYERS, SSD_CONV_DIM), 0.02),
        "ssd_a_log": jnp.log(jax.random.uniform(ks[16], (N_SSD_LAYERS, 2, SSD_HEADS), f32, 1.0, 16.0)),
        "ssd_dt_bias": dt0 + jnp.log(-jnp.expm1(-dt0)),
        "ssd_d": 1.0 + nrm(ks[18], (N_SSD_LAYERS, SSD_HEADS), 0.02),
        "ssd_norm_w": 1.0 + nrm(ks[19], (N_SSD_LAYERS, SSD_D_INNER), 0.02),
        "ssd_out_w": nrm(ks[20], (N_SSD_LAYERS, SSD_D_INNER, D_MODEL), SSD_D_INNER ** -0.5),
        "cm_pw1": nrm(ks[21], (N_CONV_LAYERS, D_MODEL, 2 * D_MODEL), D_MODEL ** -0.5),
        "cm_dw_w": nrm(ks[22], (N_CONV_LAYERS, CM_KERNEL, D_MODEL), CM_KERNEL ** -0.5),
        "cm_dw_b": nrm(ks[23], (N_CONV_LAYERS, D_MODEL), 0.02),
        "cm_ln_w": 1.0 + nrm(ks[24], (N_CONV_LAYERS, D_MODEL), 0.02),
        "cm_ln_b": nrm(ks[25], (N_CONV_LAYERS, D_MODEL), 0.02),
        "cm_pw2": nrm(ks[26], (N_CONV_LAYERS, D_MODEL, D_MODEL), D_MODEL ** -0.5),
    }


def reference(x_prompt, x_sample, state_ssd, c, c_ctx, norm_w, ada_w, ada_b, ffn_wg, ffn_wu,
              ffn_wd, pool_w, pool_scale, ssd_in_w, ssd_conv_w, ssd_conv_b, ssd_a_log,
              ssd_dt_bias, ssd_d, ssd_norm_w, ssd_out_w, cm_pw1, cm_dw_w, cm_dw_b, cm_ln_w,
              cm_ln_b, cm_pw2):
    params = (norm_w, ada_w, ada_b, ffn_wg, ffn_wu, ffn_wd, pool_w, pool_scale, ssd_in_w,
              ssd_conv_w, ssd_conv_b, ssd_a_log, ssd_dt_bias, ssd_d, ssd_norm_w, ssd_out_w,
              cm_pw1, cm_dw_w, cm_dw_b, cm_ln_w, cm_ln_b, cm_pw2)
    y_prompt, ctx_states = _trunk(x_prompt, c_ctx[None, :], None, None, *params)
    new_state_ssd = jnp.stack(ctx_states, axis=1).astype(x_prompt.dtype)
    rows = x_sample.shape[1] // GRID_W
    y_sample, _ = _trunk(x_sample, c, rows, state_ssd, *params)
    return (y_prompt, y_sample, new_state_ssd)
```

```python
import functools

import numpy as np
import jax
import jax.numpy as jnp
from jax import lax
from jax.experimental import pallas as pl
from jax.experimental.pallas import tpu as pltpu

F32 = jnp.float32
BF16 = jnp.bfloat16

D = 2048
DEPTH = 4
N_MOD = 9
D_FF = 5632
EPS = 1e-6
GRID_W = 64
POOL_WINDOWS = (2, 4, 8, 16)
POOL_GROUP_DIM = D // 4
SSD_D_INNER = 2 * D
SSD_HEAD_DIM = 64
SSD_HEADS = SSD_D_INNER // SSD_HEAD_DIM
SSD_GROUPS = 8
SSD_HEADS_PER_GROUP = SSD_HEADS // SSD_GROUPS
SSD_GROUP_DIM = SSD_HEADS_PER_GROUP * SSD_HEAD_DIM
SSD_STATE = 128
SSD_CONV = 5
SSD_CHUNK = 128
SSD_BC_DIM = SSD_GROUPS * SSD_STATE
SSD_ZX_DIM = 2 * SSD_D_INNER + 2 * SSD_BC_DIM
CM_KERNEL = 31
CM_HALO = 16
SSD_HALO = 8
POOL_CHUNK = 256

MIB = 1024 * 1024


def _silu(x):
    return x * (1.0 / (1.0 + jnp.exp(-x)))


def _sigmoid(x):
    return 1.0 / (1.0 + jnp.exp(-x))


def _rms(x, w):
    return x * lax.rsqrt(jnp.mean(x * x, axis=-1, keepdims=True) + EPS) * w


def _pre(x, nw_ref, mod_ref):
    return _rms(x, nw_ref[0:1, :]) * (1.0 + mod_ref[1, 0]) + mod_ref[0, 0]


def _dot(a, b):
    return jnp.dot(a, b, preferred_element_type=F32)


def _dot_nt(a, b):
    return lax.dot_general(a, b, (((1,), (1,)), ((), ())), preferred_element_type=F32)


def _dot_tn(a, b):
    return lax.dot_general(a, b, (((0,), (0,)), ((), ())), preferred_element_type=F32)


def _params(semantics, vmem_mib):
    return pltpu.CompilerParams(dimension_semantics=semantics,
                                vmem_limit_bytes=vmem_mib * MIB)


class _Trunk:
    def __init__(self, nseq, seq_len, shared_cond):
        self.nseq = nseq
        self.seq_len = seq_len
        self.tokens = nseq * seq_len
        self.shared_cond = shared_cond

    def cond_row(self, tm):
        if self.shared_cond:
            return lambda i: 0
        per = self.seq_len // tm
        return lambda i: i // per

    def mod_spec(self, tm):
        row = self.cond_row(tm)
        return pl.BlockSpec((3, 1, 1, D), lambda i, *_: (0, row(i), 0, 0))


def _ada_kernel(c_ref, w_ref, b_ref, o_ref):
    sc = _silu(c_ref[...]).astype(BF16)
    o_ref[0] = _dot(sc, w_ref[0].astype(BF16)) + b_ref[0]


def _ada_mod(cond8, ada_w, ada_b):
    tn = 1024
    n = N_MOD * D
    return pl.pallas_call(
        _ada_kernel,
        grid=(DEPTH, n // tn),
        in_specs=[
            pl.BlockSpec((8, D), lambda l, j: (0, 0)),
            pl.BlockSpec((1, D, tn), lambda l, j: (l, 0, j)),
            pl.BlockSpec((1, 1, tn), lambda l, j: (l, 0, j)),
        ],
        out_specs=pl.BlockSpec((1, 8, tn), lambda l, j: (l, 0, j)),
        out_shape=jax.ShapeDtypeStruct((DEPTH, 8, n), F32),
        compiler_params=_params(("arbitrary", "arbitrary"), 40),
    )(cond8, ada_w, ada_b.reshape(DEPTH, 1, n))


def _ffn_kernel(x_ref, mod_ref, nw_ref, wg_ref, wu_ref, wd_ref, o_ref, h_ref, acc_ref):
    j = pl.program_id(1)

    @pl.when(j == 0)
    def _():
        h_ref[...] = _pre(x_ref[...], nw_ref, mod_ref).astype(BF16)
        acc_ref[...] = jnp.zeros_like(acc_ref)

    h = h_ref[...]
    g = _dot(h, wg_ref[...])
    u = _dot(h, wu_ref[...])
    a = (_silu(g) * u).astype(BF16)
    acc_ref[...] += _dot(a, wd_ref[...])

    @pl.when(j == pl.num_programs(1) - 1)
    def _():
        r = _rms(acc_ref[...], nw_ref[1:2, :])
        o_ref[...] = x_ref[...] + (0.5 * mod_ref[2, 0]) * r


def _ffn(trunk, x, mod3, nw, wg, wu, wd):
    tm, tf = 512, 512
    return pl.pallas_call(
        _ffn_kernel,
        grid=(trunk.tokens // tm, D_FF // tf),
        in_specs=[
            pl.BlockSpec((tm, D), lambda i, j: (i, 0)),
            trunk.mod_spec(tm),
            pl.BlockSpec((2, D), lambda i, j: (0, 0)),
            pl.BlockSpec((D, tf), lambda i, j: (0, j)),
            pl.BlockSpec((D, tf), lambda i, j: (0, j)),
            pl.BlockSpec((tf, D), lambda i, j: (j, 0)),
        ],
        out_specs=pl.BlockSpec((tm, D), lambda i, j: (i, 0)),
        out_shape=jax.ShapeDtypeStruct((trunk.tokens, D), F32),
        scratch_shapes=[pltpu.VMEM((tm, D), BF16), pltpu.VMEM((tm, D), F32)],
        compiler_params=_params(("arbitrary", "arbitrary"), 48),
    )(x, mod3, nw, wg, wu, wd)


def _window_bounds(n, w):
    pos = np.arange(n)
    lo = np.clip(pos - w // 2, 0, n)
    hi = np.clip(pos + (w - w // 2), 0, n)
    return lo, hi


def _band_matrix(n, w):
    lo, hi = _window_bounds(n, w)
    col = np.arange(n)[None, :]
    return ((col >= lo[:, None]) & (col < hi[:, None])).astype(np.float32)


def _split_bf16(v):
    hi = v.astype(BF16)
    lo = (v - hi.astype(F32)).astype(BF16)
    return hi, lo


def _window_sum(a, v):
    hi, lo = _split_bf16(v)
    return _dot(a, hi) + _dot(a, lo)


def _pool_finish(x, mix_ref, mod_ref, nw_ref, o_ref):
    o_ref[...] = x + mod_ref[2, 0] * _rms(mix_ref[...], nw_ref[1:2, :])


def _inv_rms(x_ref):
    x = x_ref[...]
    return lax.rsqrt(jnp.mean(x * x, axis=-1, keepdims=True) + EPS)


def _pre_slice(x_ref, inv, nw_ref, mod_ref, rows, cols):
    y = x_ref[rows, cols] * inv[rows, :] * nw_ref[0:1, cols]
    return y * (1.0 + mod_ref[1, 0, :, cols]) + mod_ref[0, 0, :, cols]


def _pool_ctx_kernel(x_ref, mod_ref, nw_ref, a_ref, ic_ref, pw_ref, ps_ref, o_ref, mix_ref, *, tm):
    inv = _inv_rms(x_ref)
    gd = POOL_GROUP_DIM
    for g in range(len(POOL_WINDOWS)):
        cols = slice(g * gd, (g + 1) * gd)
        a = a_ref[g]
        for q in range(tm // POOL_CHUNK):
            rows = slice(q * POOL_CHUNK, (q + 1) * POOL_CHUNK)
            v = _pre_slice(x_ref, inv, nw_ref, mod_ref, rows, cols)
            d = (_window_sum(a, v) * ic_ref[g] - v).astype(BF16)
            mix_ref[rows, cols] = _dot(d, pw_ref[g]) * ps_ref[:, cols]
    _pool_finish(x_ref[...], mix_ref, mod_ref, nw_ref, o_ref)


def _pool_lat_kernel(xp_ref, xc_ref, xn_ref, mod_ref, nw_ref, a_ref, ic_ref, pw_ref, ps_ref,
                     o_ref, mix_ref, *, blocks_per_seq):
    jb = pl.program_id(0) % blocks_per_seq
    valid_p = jb > 0
    valid_n = jb < blocks_per_seq - 1
    x_refs = (xp_ref, xc_ref, xn_ref)
    invs = tuple(_inv_rms(r) for r in x_refs)
    gd = POOL_GROUP_DIM
    rows_per_chunk = POOL_CHUNK // GRID_W
    rows_per_block = xc_ref.shape[0] // GRID_W
    chunks_per_block = rows_per_block // rows_per_chunk
    for g, w in enumerate(POOL_WINDOWS):
        before, after = w // 2, w - w // 2 - 1
        cols = slice(g * gd, (g + 1) * gd)
        a = a_ref[g]
        first_row = rows_per_block - before
        last_row = 2 * rows_per_block - 1 + after
        cm = {}
        for ch in range(first_row // rows_per_chunk, last_row // rows_per_chunk + 1):
            which = ch // chunks_per_block
            off = (ch % chunks_per_block) * POOL_CHUNK
            v = _pre_slice(x_refs[which], invs[which], nw_ref, mod_ref,
                           slice(off, off + POOL_CHUNK), cols)
            s = _window_sum(a, v)
            if ch < chunks_per_block:
                s = jnp.where(valid_p, s, 0.0)
            elif ch >= 2 * chunks_per_block:
                s = jnp.where(valid_n, s, 0.0)
            cm[ch] = s

        def slab(r):
            o = (r % rows_per_chunk) * GRID_W
            return cm[r // rows_per_chunk][o:o + GRID_W, :]

        outs = []
        for r in range(rows_per_block, 2 * rows_per_block):
            acc = slab(r - before)
            for k in range(-before + 1, after + 1):
                acc = acc + slab(r + k)
            outs.append(acc)
        m = jnp.concatenate(outs, axis=0) * ic_ref[0, g]
        v = _pre_slice(xc_ref, invs[1], nw_ref, mod_ref, slice(None), cols)
        d = (m - v).astype(BF16)
        mix_ref[:, cols] = _dot(d, pw_ref[g]) * ps_ref[:, cols]
    _pool_finish(xc_ref[...], mix_ref, mod_ref, nw_ref, o_ref)


def _pool_ctx_consts(seq_len):
    mats, inv = [], []
    for w in POOL_WINDOWS:
        lo, hi = _window_bounds(seq_len, w)
        mats.append(_band_matrix(seq_len, w))
        inv.append((1.0 / (hi - lo)).astype(np.float32)[:, None])
    return jnp.asarray(np.stack(mats), BF16), jnp.asarray(np.stack(inv), F32)


def _pool_lat_consts(rows, tm):
    rows_per_block = tm // GRID_W
    mats, inv = [], []
    for w in POOL_WINDOWS:
        mats.append(np.kron(np.eye(POOL_CHUNK // GRID_W, dtype=np.float32), _band_matrix(GRID_W, w)))
        lo_c, hi_c = _window_bounds(GRID_W, w)
        lo_r, hi_r = _window_bounds(rows, w)
        cnt = (hi_r - lo_r)[:, None] * (hi_c - lo_c)[None, :]
        inv.append((1.0 / cnt).astype(np.float32).reshape(rows // rows_per_block, tm, 1))
    inv = np.stack(inv, axis=1)
    return jnp.asarray(np.stack(mats), BF16), jnp.asarray(inv, F32)


def _pool_mixer(trunk, x, mod3, nw, pool_w, pool_scale, grid_rows):
    tm = 512
    nblk = trunk.tokens // tm
    common_specs = [
        trunk.mod_spec(tm),
        pl.BlockSpec((2, D), lambda i: (0, 0)),
        pl.BlockSpec((4, POOL_CHUNK, POOL_CHUNK), lambda i: (0, 0, 0)),
    ]
    tail_specs = [
        pl.BlockSpec((4, POOL_GROUP_DIM, POOL_GROUP_DIM), lambda i: (0, 0, 0)),
        pl.BlockSpec((1, D), lambda i: (0, 0)),
    ]
    out_spec = pl.BlockSpec((tm, D), lambda i: (i, 0))
    out_shape = jax.ShapeDtypeStruct((trunk.tokens, D), F32)
    scratch = [pltpu.VMEM((tm, D), F32)]
    if grid_rows is None:
        assert trunk.seq_len == POOL_CHUNK
        a, ic = _pool_ctx_consts(trunk.seq_len)
        return pl.pallas_call(
            functools.partial(_pool_ctx_kernel, tm=tm),
            grid=(nblk,),
            in_specs=[pl.BlockSpec((tm, D), lambda i: (i, 0))] + common_specs
            + [pl.BlockSpec((4, POOL_CHUNK, 1), lambda i: (0, 0, 0))] + tail_specs,
            out_specs=out_spec, out_shape=out_shape, scratch_shapes=scratch,
            compiler_params=_params(("arbitrary",), 48),
        )(x, mod3, nw, a, ic, pool_w, pool_scale)
    blocks_per_seq = trunk.seq_len // tm
    assert max(POOL_WINDOWS) // 2 <= tm // GRID_W
    a, ic = _pool_lat_consts(grid_rows, tm)
    return pl.pallas_call(
        functools.partial(_pool_lat_kernel, blocks_per_seq=blocks_per_seq),
        grid=(nblk,),
        in_specs=[
            pl.BlockSpec((tm, D), lambda i: (jnp.maximum(i - 1, 0), 0)),
            pl.BlockSpec((tm, D), lambda i: (i, 0)),
            pl.BlockSpec((tm, D), lambda i: (jnp.minimum(i + 1, nblk - 1), 0)),
        ] + common_specs
        + [pl.BlockSpec((1, 4, tm, 1), lambda i: (i % blocks_per_seq, 0, 0, 0))] + tail_specs,
        out_specs=out_spec, out_shape=out_shape, scratch_shapes=scratch,
        compiler_params=_params(("arbitrary",), 56),
    )(x, x, x, mod3, nw, a, ic, pool_w, pool_scale)


def _ssd_in_kernel(x_ref, mod_ref, nw_ref, w_ref, wdt_ref, zx_ref, dt_ref, h_ref):
    @pl.when(pl.program_id(1) == 0)
    def _():
        h = _pre(x_ref[...], nw_ref, mod_ref).astype(BF16)
        h_ref[...] = h
        dt_ref[...] = _dot(h, wdt_ref[...])

    zx_ref[...] = _dot(h_ref[...], w_ref[...])


def _ssd_in(trunk, x, mod3, nw, w_zx, w_dt):
    tm, tn = 512, 1024
    return pl.pallas_call(
        _ssd_in_kernel,
        grid=(trunk.tokens // tm, SSD_ZX_DIM // tn),
        in_specs=[
            pl.BlockSpec((tm, D), lambda i, j: (i, 0)),
            trunk.mod_spec(tm),
            pl.BlockSpec((2, D), lambda i, j: (0, 0)),
            pl.BlockSpec((D, tn), lambda i, j: (0, j)),
            pl.BlockSpec((D, 2 * SSD_HEADS), lambda i, j: (0, 0)),
        ],
        out_specs=[
            pl.BlockSpec((tm, tn), lambda i, j: (i, j)),
            pl.BlockSpec((tm, 2 * SSD_HEADS), lambda i, j: (i, 0)),
        ],
        out_shape=[
            jax.ShapeDtypeStruct((trunk.tokens, SSD_ZX_DIM), F32),
            jax.ShapeDtypeStruct((trunk.tokens, 2 * SSD_HEADS), F32),
        ],
        scratch_shapes=[pltpu.VMEM((tm, D), BF16)],
        compiler_params=_params(("arbitrary", "arbitrary"), 40),
    )(x, mod3, nw, w_zx, w_dt)


def _causal_conv_silu(buf_ref, w_ref, b_ref, rows):
    k = w_ref.shape[0]
    base = SSD_HALO - k // 2
    acc = b_ref[...] + w_ref[0:1, :] * buf_ref[base:base + rows, :]
    for t in range(1, k):
        acc = acc + w_ref[t:t + 1, :] * buf_ref[base + t:base + t + rows, :]
    return _silu(acc)


def _ssd_scan_kernel(*refs, reverse, nc, has_h0, emit_state, final):
    it = iter(refs)
    xs_ref, xsp_ref, xsn_ref = next(it), next(it), next(it)
    bm_ref, bmp_ref, bmn_ref = next(it), next(it), next(it)
    cm_ref, cmp_ref, cmn_ref = next(it), next(it), next(it)
    dt_ref = next(it)
    cwx_ref, cwb_ref, cwc_ref = next(it), next(it), next(it)
    cbx_ref, cbb_ref, cbc_ref = next(it), next(it), next(it)
    dtb_ref, alog_ref = next(it), next(it)
    h0_ref = next(it) if has_h0 else None
    if final:
        yf_ref, z_ref, gn_ref = next(it), next(it), next(it)
    else:
        dskip_ref = next(it)
    y_ref = next(it)
    so_ref = next(it) if emit_state else None
    (state_ref, xbuf_ref, bbuf_ref, cbuf_ref, xs_s, b_s, c_s, acol_s, arow_s, dt_s, y_s) = it

    t = SSD_CHUNK
    step = pl.program_id(1)
    cc = (nc - 1 - step) if reverse else step
    has_prev = cc > 0
    has_next = cc < nc - 1

    @pl.when(step == 0)
    def _():
        if has_h0:
            state_ref[...] = h0_ref[0, 0]
        else:
            state_ref[...] = jnp.zeros_like(state_ref)

    def stage(buf, cur, prev, nxt):
        buf[0:SSD_HALO, :] = jnp.where(has_prev, prev[...], 0.0)
        buf[SSD_HALO:SSD_HALO + t, :] = cur[...]
        buf[SSD_HALO + t:SSD_HALO + t + SSD_HALO, :] = jnp.where(has_next, nxt[...], 0.0)

    stage(xbuf_ref, xs_ref, xsp_ref, xsn_ref)
    stage(bbuf_ref, bm_ref, bmp_ref, bmn_ref)
    stage(cbuf_ref, cm_ref, cmp_ref, cmn_ref)
    xc = _causal_conv_silu(xbuf_ref, cwx_ref, cbx_ref, t)
    bc = _causal_conv_silu(bbuf_ref, cwb_ref, cbb_ref, t)
    ccv = _causal_conv_silu(cbuf_ref, cwc_ref, cbc_ref, t)

    dt_raw = dt_ref[...] + dtb_ref[...]
    dt = jnp.maximum(dt_raw, 0.0) + jnp.log1p(jnp.exp(-jnp.abs(dt_raw)))
    dta = dt * (-jnp.exp(alog_ref[...]))

    row = lax.broadcasted_iota(jnp.int32, (t, t), 0)
    col = lax.broadcasted_iota(jnp.int32, (t, t), 1)
    mask = (col >= row) if reverse else (col <= row)
    acum = jnp.dot(mask.astype(F32), dta, precision=lax.Precision.HIGHEST,
                   preferred_element_type=F32)
    acum_t = acum.T
    end = 0 if reverse else t - 1
    d0 = SSD_HEADS if reverse else 0

    hpg = SSD_HEADS_PER_GROUP
    for g in range(SSD_GROUPS):
        heads = slice(d0 + g * hpg, d0 + (g + 1) * hpg)
        xs_s[g] = xc[:, g * SSD_GROUP_DIM:(g + 1) * SSD_GROUP_DIM]
        b_s[g] = bc[:, g * SSD_STATE:(g + 1) * SSD_STATE]
        c_s[g] = ccv[:, g * SSD_STATE:(g + 1) * SSD_STATE]
        acol_s[g] = acum[:, heads]
        arow_s[g] = acum_t[heads, :]
        dt_s[g] = dt[:, heads]

    pair = 2 * SSD_HEAD_DIM
    first_head = lax.broadcasted_iota(jnp.int32, (t, pair), 1) < SSD_HEAD_DIM

    def group_body(g, carry):
        xg = xs_s[g]
        bg = b_s[g].astype(BF16)
        cg = c_s[g].astype(BF16)
        acol = acol_s[g]
        arow = arow_s[g]
        dtg = dt_s[g]
        cb = _dot_nt(cg, bg)
        r0 = pl.multiple_of(g * SSD_GROUP_DIM, SSD_GROUP_DIM)
        st = state_ref[pl.ds(r0, SSD_GROUP_DIM), :]
        y_off = _dot_nt(cg, st.astype(BF16))
        ys, xes, decs = [], [], []
        for p in range(hpg // 2):
            lanes = slice(p * pair, (p + 1) * pair)
            a0, a1 = acol[:, 2 * p:2 * p + 1], acol[:, 2 * p + 1:2 * p + 2]
            e0, e1 = a0[end:end + 1, :], a1[end:end + 1, :]
            x_dt = xg[:, lanes] * jnp.where(first_head, dtg[:, 2 * p:2 * p + 1],
                                            dtg[:, 2 * p + 1:2 * p + 2])
            l0 = jnp.exp(jnp.where(mask, a0 - arow[2 * p:2 * p + 1, :], -jnp.inf))
            l1 = jnp.exp(jnp.where(mask, a1 - arow[2 * p + 1:2 * p + 2, :], -jnp.inf))
            scores = jnp.concatenate([(cb * l0).astype(BF16), (cb * l1).astype(BF16)], axis=1)
            x_bd = jnp.concatenate([jnp.where(first_head, x_dt, 0.0),
                                    jnp.where(first_head, 0.0, x_dt)], axis=0).astype(BF16)
            y_diag = _dot(scores, x_bd)
            ys.append(y_diag + y_off[:, lanes] * jnp.where(first_head, jnp.exp(a0), jnp.exp(a1)))
            xes.append(x_dt * jnp.where(first_head, jnp.exp(e0 - a0), jnp.exp(e1 - a1)))
            decs.append(jnp.broadcast_to(jnp.exp(e0), (SSD_HEAD_DIM, 1)))
            decs.append(jnp.broadcast_to(jnp.exp(e1), (SSD_HEAD_DIM, 1)))
        y_s[g] = jnp.concatenate(ys, axis=1)
        x_end = jnp.concatenate(xes, axis=1).astype(BF16)
        new = _dot_tn(x_end, bg)
        state_ref[pl.ds(r0, SSD_GROUP_DIM), :] = st * jnp.concatenate(decs, axis=0) + new
        return carry

    lax.fori_loop(0, SSD_GROUPS, group_body, 0)

    if final:
        for g in range(SSD_GROUPS):
            lanes = slice(g * SSD_GROUP_DIM, (g + 1) * SSD_GROUP_DIM)
            xbuf_ref[0:t, lanes] = (yf_ref[:, lanes] + y_s[g]) * _silu(z_ref[:, lanes])
        y_ref[...] = _rms(xbuf_ref[0:t, :], gn_ref[...]).astype(y_ref.dtype)
    else:
        for g in range(SSD_GROUPS):
            lanes = slice(g * SSD_GROUP_DIM, (g + 1) * SSD_GROUP_DIM)
            y_ref[:, lanes] = y_s[g] + dskip_ref[:, lanes] * xs_s[g]

    if emit_state:
        @pl.when(step == nc - 1)
        def _():
            so_ref[0] = state_ref[...]


def _ssd_scan(trunk, zx, dt, conv_w, conv_b, dt_bias, a_log, *, reverse, h0=None, d_skip=None,
              y_fwd=None, gnorm=None):
    t = SSD_CHUNK
    nc = trunk.seq_len // t
    final = y_fwd is not None
    emit_state = h0 is None
    hb = t // SSD_HALO
    last8 = trunk.tokens // SSD_HALO - 1
    xs_col = SSD_D_INNER // SSD_D_INNER
    b_col = (2 * SSD_D_INNER) // SSD_BC_DIM
    c_col = (2 * SSD_D_INNER + SSD_BC_DIM) // SSD_BC_DIM

    def blk(b, c):
        return b * nc + ((nc - 1 - c) if reverse else c)

    def cur(width, colb):
        return pl.BlockSpec((t, width), lambda b, c: (blk(b, c), colb))

    def prev(width, colb):
        return pl.BlockSpec((SSD_HALO, width), lambda b, c: (jnp.maximum(blk(b, c) * hb - 1, 0), colb))

    def nxt(width, colb):
        return pl.BlockSpec((SSD_HALO, width),
                            lambda b, c: (jnp.minimum(blk(b, c) * hb + hb, last8), colb))

    def full(shape):
        return pl.BlockSpec(shape, lambda b, c: (0,) * len(shape))

    di, bc = SSD_D_INNER, SSD_BC_DIM
    inputs, specs = [], []

    def add(arr, spec):
        inputs.append(arr)
        specs.append(spec)

    for width, colb in ((di, xs_col), (bc, b_col), (bc, c_col)):
        add(zx, cur(width, colb))
        add(zx, prev(width, colb))
        add(zx, nxt(width, colb))
    add(dt, pl.BlockSpec((t, 2 * SSD_HEADS), lambda b, c: (blk(b, c), 0)))
    for lo, hi in ((0, di), (di, di + bc), (di + bc, di + 2 * bc)):
        add(conv_w[:, lo:hi], full((SSD_CONV, hi - lo)))
    for lo, hi in ((0, di), (di, di + bc), (di + bc, di + 2 * bc)):
        add(conv_b[None, lo:hi], full((1, hi - lo)))
    add(dt_bias.reshape(1, 2 * SSD_HEADS), full((1, 2 * SSD_HEADS)))
    add(a_log.reshape(1, 2 * SSD_HEADS), full((1, 2 * SSD_HEADS)))
    if h0 is not None:
        d = 1 if reverse else 0
        add(h0, pl.BlockSpec((1, 1, SSD_D_INNER, SSD_STATE), lambda b, c: (b, d, 0, 0)))
    if final:
        add(y_fwd, cur(di, 0))
        add(zx, cur(di, 0))
        add(gnorm.reshape(1, di), full((1, di)))
    else:
        add(jnp.repeat(d_skip, SSD_HEAD_DIM).reshape(1, di), full((1, di)))

    out_specs = [cur(di, 0)]
    out_shape = [jax.ShapeDtypeStruct((trunk.tokens, di), BF16 if final else F32)]
    if emit_state:
        out_specs.append(pl.BlockSpec((1, di, SSD_STATE), lambda b, c: (b, 0, 0)))
        out_shape.append(jax.ShapeDtypeStruct((trunk.nseq, di, SSD_STATE), F32))

    hpg = SSD_HEADS_PER_GROUP
    rows = t + 2 * SSD_HALO
    scratch = [
        pltpu.VMEM((di, SSD_STATE), F32),
        pltpu.VMEM((rows, di), F32),
        pltpu.VMEM((rows, bc), F32),
        pltpu.VMEM((rows, bc), F32),
        pltpu.VMEM((SSD_GROUPS, t, SSD_GROUP_DIM), F32),
        pltpu.VMEM((SSD_GROUPS, t, SSD_STATE), F32),
        pltpu.VMEM((SSD_GROUPS, t, SSD_STATE), F32),
        pltpu.VMEM((SSD_GROUPS, t, hpg), F32),
        pltpu.VMEM((SSD_GROUPS, hpg, t), F32),
        pltpu.VMEM((SSD_GROUPS, t, hpg), F32),
        pltpu.VMEM((SSD_GROUPS, t, SSD_GROUP_DIM), F32),
    ]
    res = pl.pallas_call(
        functools.partial(_ssd_scan_kernel, reverse=reverse, nc=nc, has_h0=h0 is not None,
                          emit_state=emit_state, final=final),
        grid=(trunk.nseq, nc),
        in_specs=specs, out_specs=out_specs, out_shape=out_shape, scratch_shapes=scratch,
        compiler_params=_params(("arbitrary", "arbitrary"), 48),
    )(*inputs)
    return res if emit_state else (res[0], None)


def _mm_post_kernel(a_ref, w_ref, x_ref, mod_ref, nw_ref, o_ref, acc_ref):
    k = pl.program_id(1)

    @pl.when(k == 0)
    def _():
        acc_ref[...] = jnp.zeros_like(acc_ref)

    acc_ref[...] += _dot(a_ref[...], w_ref[...])

    @pl.when(k == pl.num_programs(1) - 1)
    def _():
        o_ref[...] = x_ref[...] + mod_ref[2, 0] * _rms(acc_ref[...], nw_ref[1:2, :])


def _mm_post(trunk, a, w, x, mod3, nw):
    tm, tk = 512, 1024
    kdim = a.shape[1]
    return pl.pallas_call(
        _mm_post_kernel,
        grid=(trunk.tokens // tm, kdim // tk),
        in_specs=[
            pl.BlockSpec((tm, tk), lambda i, k: (i, k)),
            pl.BlockSpec((tk, D), lambda i, k: (k, 0)),
            pl.BlockSpec((tm, D), lambda i, k: (i, 0)),
            trunk.mod_spec(tm),
            pl.BlockSpec((2, D), lambda i, k: (0, 0)),
        ],
        out_specs=pl.BlockSpec((tm, D), lambda i, k: (i, 0)),
        out_shape=jax.ShapeDtypeStruct((trunk.tokens, D), F32),
        scratch_shapes=[pltpu.VMEM((tm, D), F32)],
        compiler_params=_params(("arbitrary", "arbitrary"), 40),
    )(a, w, x, mod3, nw)


def _ssd_mixer(trunk, x, mod3, nw, h0, p):
    zx, dt = _ssd_in(trunk, x, mod3, nw, p["w_zx"], p["w_dt"])
    common = (p["conv_w"], p["conv_b"], p["dt_bias"], p["a_log"])
    y_f, s_f = _ssd_scan(trunk, zx, dt, *common, reverse=False, h0=h0, d_skip=p["d_skip"])
    y_n, s_b = _ssd_scan(trunk, zx, dt, *common, reverse=True, h0=h0, y_fwd=y_f, gnorm=p["gnorm"])
    return _mm_post(trunk, y_n, p["out_w"], x, mod3, nw), s_f, s_b


def _cm_in_kernel(x_ref, mod_ref, nw_ref, wa_ref, wb_ref, o_ref, h_ref):
    @pl.when(pl.program_id(1) == 0)
    def _():
        h_ref[...] = _pre(x_ref[...], nw_ref, mod_ref).astype(BF16)

    h = h_ref[...]
    o_ref[...] = _dot(h, wa_ref[...]) * _sigmoid(_dot(h, wb_ref[...]))


def _cm_in(trunk, x, mod3, nw, pw1):
    tm, tn = 512, 512
    nb = D // tn
    return pl.pallas_call(
        _cm_in_kernel,
        grid=(trunk.tokens // tm, nb),
        in_specs=[
            pl.BlockSpec((tm, D), lambda i, j: (i, 0)),
            trunk.mod_spec(tm),
            pl.BlockSpec((2, D), lambda i, j: (0, 0)),
            pl.BlockSpec((D, tn), lambda i, j: (0, j)),
            pl.BlockSpec((D, tn), lambda i, j: (0, j + nb)),
        ],
        out_specs=pl.BlockSpec((tm, tn), lambda i, j: (i, j)),
        out_shape=jax.ShapeDtypeStruct((trunk.tokens, D), F32),
        scratch_shapes=[pltpu.VMEM((tm, D), BF16)],
        compiler_params=_params(("arbitrary", "arbitrary"), 40),
    )(x, mod3, nw, pw1, pw1)


def _cm_out_kernel(u_ref, up_ref, un_ref, dw_ref, db_ref, lw_ref, lb_ref, w2_ref, x_ref, mod_ref,
                   nw_ref, o_ref, buf_ref, cv_ref, *, blocks_per_seq):
    tm = u_ref.shape[0]
    jb = pl.program_id(0) % blocks_per_seq
    buf_ref[0:CM_HALO, :] = jnp.where(jb > 0, up_ref[...], 0.0)
    buf_ref[CM_HALO:CM_HALO + tm, :] = u_ref[...]
    buf_ref[CM_HALO + tm:CM_HALO + tm + CM_HALO, :] = jnp.where(jb < blocks_per_seq - 1,
                                                                 un_ref[...], 0.0)
    base = CM_HALO - CM_KERNEL // 2
    rb = 16
    for r in range(0, tm, rb):
        acc = db_ref[...] + dw_ref[0:1, :] * buf_ref[base + r:base + r + rb, :]
        for t in range(1, CM_KERNEL):
            acc = acc + dw_ref[t:t + 1, :] * buf_ref[base + r + t:base + r + t + rb, :]
        cv_ref[r:r + rb, :] = acc
    acc = cv_ref[...]
    mu = jnp.mean(acc, axis=-1, keepdims=True)
    cen = acc - mu
    var = jnp.mean(cen * cen, axis=-1, keepdims=True)
    v = _silu(cen * lax.rsqrt(var + EPS) * lw_ref[...] + lb_ref[...]).astype(BF16)
    out = _dot(v, w2_ref[...])
    o_ref[...] = x_ref[...] + mod_ref[2, 0] * _rms(out, nw_ref[1:2, :])


def _cm_out(trunk, u, dw_w, dw_b, ln_w, ln_b, pw2, x, mod3, nw):
    tm = 256
    nblk = trunk.tokens // tm
    blocks_per_seq = trunk.seq_len // tm
    hb = tm // CM_HALO
    last = trunk.tokens // CM_HALO - 1
    row = pl.BlockSpec((1, D), lambda i: (0, 0))
    return pl.pallas_call(
        functools.partial(_cm_out_kernel, blocks_per_seq=blocks_per_seq),
        grid=(nblk,),
        in_specs=[
            pl.BlockSpec((tm, D), lambda i: (i, 0)),
            pl.BlockSpec((CM_HALO, D), lambda i: (jnp.maximum(i * hb - 1, 0), 0)),
            pl.BlockSpec((CM_HALO, D), lambda i: (jnp.minimum(i * hb + hb, last), 0)),
            pl.BlockSpec((CM_KERNEL, D), lambda i: (0, 0)),
            row, row, row,
            pl.BlockSpec((D, D), lambda i: (0, 0)),
            pl.BlockSpec((tm, D), lambda i: (i, 0)),
            trunk.mod_spec(tm),
            pl.BlockSpec((2, D), lambda i: (0, 0)),
        ],
        out_specs=pl.BlockSpec((tm, D), lambda i: (i, 0)),
        out_shape=jax.ShapeDtypeStruct((trunk.tokens, D), F32),
        scratch_shapes=[pltpu.VMEM((tm + 2 * CM_HALO, D), F32), pltpu.VMEM((tm, D), F32)],
        compiler_params=_params(("arbitrary",), 48),
    )(u, u, u, dw_w, dw_b.reshape(1, D), ln_w.reshape(1, D), ln_b.reshape(1, D), pw2, x, mod3, nw)


def _run_trunk(trunk, x, mod_t, h0_all, grid_rows, w):
    states = []
    for i in range(DEPTH):
        kind, j = i % 3, i // 3
        x = _ffn(trunk, x, mod_t[i, 0:3], w["norm_w"][i, 0], w["wg"][i][0], w["wu"][i][0], w["wd"][i][0])
        mod3, nw = mod_t[i, 3:6], w["norm_w"][i, 1]
        if kind == 0:
            x = _pool_mixer(trunk, x, mod3, nw, w["pool_w"][j], w["pool_scale"][j][None, :], grid_rows)
        elif kind == 1:
            h0 = None if h0_all is None else h0_all[:, j].reshape(trunk.nseq, 2, SSD_D_INNER, SSD_STATE)
            x, s_f, s_b = _ssd_mixer(trunk, x, mod3, nw, h0, w["ssd"][j])
            states.append((s_f, s_b))
        else:
            c = w["cm"][j]
            u = _cm_in(trunk, x, mod3, nw, c["pw1"])
            x = _cm_out(trunk, u, c["dw_w"], c["dw_b"], c["ln_w"], c["ln_b"], c["pw2"], x, mod3, nw)
        x = _ffn(trunk, x, mod_t[i, 6:9], w["norm_w"][i, 2], w["wg"][i][1], w["wu"][i][1], w["wd"][i][1])
    return x, states


def kernel(x_prompt, x_sample, state_ssd, c, c_ctx, norm_w, ada_w, ada_b, ffn_wg, ffn_wu, ffn_wd, pool_w, pool_scale, ssd_in_w, ssd_conv_w, ssd_conv_b, ssd_a_log, ssd_dt_bias, ssd_d, ssd_norm_w, ssd_out_w, cm_pw1, cm_dw_w, cm_dw_b, cm_ln_w, cm_ln_b, cm_pw2):
    n_ctx, l_ctx, _ = x_prompt.shape
    n_lat, l_lat, _ = x_sample.shape
    ctx = _Trunk(n_ctx, l_ctx, shared_cond=True)
    lat = _Trunk(n_lat, l_lat, shared_cond=False)

    cond8 = jnp.zeros((8, D), F32).at[0].set(c_ctx).at[1:1 + n_lat].set(c)
    mod = _ada_mod(cond8, ada_w, ada_b).reshape(DEPTH, 8, N_MOD, D)
    mod = jnp.transpose(mod, (0, 2, 1, 3))[:, :, :, None, :]
    mod_ctx, mod_lat = mod[:, :, 0:1], mod[:, :, 1:1 + n_lat]

    n_ssd = ssd_in_w.shape[0]
    w = {
        "norm_w": norm_w,
        "wg": [[ffn_wg[i, k].astype(BF16) for k in range(2)] for i in range(DEPTH)],
        "wu": [[ffn_wu[i, k].astype(BF16) for k in range(2)] for i in range(DEPTH)],
        "wd": [[ffn_wd[i, k].astype(BF16) for k in range(2)] for i in range(DEPTH)],
        "pool_w": pool_w.astype(BF16),
        "pool_scale": pool_scale,
        "ssd": [dict(w_zx=ssd_in_w[j, :, :SSD_ZX_DIM].astype(BF16),
                     w_dt=ssd_in_w[j, :, SSD_ZX_DIM:].astype(BF16),
                     conv_w=ssd_conv_w[j], conv_b=ssd_conv_b[j], dt_bias=ssd_dt_bias[j],
                     a_log=ssd_a_log[j], d_skip=ssd_d[j], gnorm=ssd_norm_w[j],
                     out_w=ssd_out_w[j].astype(BF16)) for j in range(n_ssd)],
        "cm": [dict(pw1=cm_pw1[j].astype(BF16), dw_w=cm_dw_w[j], dw_b=cm_dw_b[j], ln_w=cm_ln_w[j],
                    ln_b=cm_ln_b[j], pw2=cm_pw2[j].astype(BF16)) for j in range(cm_pw1.shape[0])],
    }

    y_ctx, ctx_states = _run_trunk(ctx, x_prompt.reshape(ctx.tokens, D), mod_ctx, None, None, w)
    y_lat, _ = _run_trunk(lat, x_sample.reshape(lat.tokens, D), mod_lat, state_ssd,
                          l_lat // GRID_W, w)

    new_state = jnp.stack([jnp.stack([s_f, s_b], axis=1) for s_f, s_b in ctx_states], axis=1)
    new_state = new_state.reshape(n_ctx, n_ssd, 2, SSD_HEADS, SSD_HEAD_DIM, SSD_STATE)
    return (y_ctx.reshape(x_prompt.shape), y_lat.reshape(x_sample.shape),
            new_state.astype(x_prompt.dtype))
```

```python
import functools

import numpy as np
import jax
import jax.numpy as jnp
from jax import lax
from jax.experimental import pallas as pl
from jax.experimental.pallas import tpu as pltpu

F32 = jnp.float32
BF16 = jnp.bfloat16

D = 2048
DEPTH = 4
N_MOD = 9
D_FF = 5632
EPS = 1e-6
GRID_W = 64
POOL_WINDOWS = (2, 4, 8, 16)
POOL_GROUP_DIM = D // 4
SSD_D_INNER = 2 * D
SSD_HEAD_DIM = 64
SSD_HEADS = SSD_D_INNER // SSD_HEAD_DIM
SSD_GROUPS = 8
SSD_HEADS_PER_GROUP = SSD_HEADS // SSD_GROUPS
SSD_GROUP_DIM = SSD_HEADS_PER_GROUP * SSD_HEAD_DIM
SSD_STATE = 128
SSD_CONV = 5
SSD_CHUNK = 128
SSD_BC_DIM = SSD_GROUPS * SSD_STATE
SSD_ZX_DIM = 2 * SSD_D_INNER + 2 * SSD_BC_DIM
SSD_HALO = 8
SSD_STRIP = SSD_GROUP_DIM
SSD_X_STRIPS = SSD_D_INNER // SSD_STRIP
SSD_BC_STRIPS = 2 * SSD_BC_DIM // SSD_STRIP
CM_KERNEL = 31
CM_HALO = 16
CM_STRIP = 512
POOL_CHUNK = 256
SUBLANES = 8
LANES = 128
NORM_ROWS = 16

MIB = 1024 * 1024


def _silu(x):
    return x * (1.0 / (1.0 + jnp.exp(-x)))


def _sigmoid(x):
    return 1.0 / (1.0 + jnp.exp(-x))


def _rms(x, w):
    return x * lax.rsqrt(jnp.mean(x * x, axis=-1, keepdims=True) + EPS) * w


def _row_loop(n_rows, rows, body):
    def step(i, carry):
        body(pl.ds(pl.multiple_of(i * rows, rows), rows))
        return carry

    lax.fori_loop(0, n_rows // rows, step, 0, unroll=2)


def _inv_rms_to(src_ref, inv_ref):
    x = src_ref[...]
    inv = lax.rsqrt(jnp.mean(x * x, axis=-1, keepdims=True) + EPS)
    inv_ref[...] = jnp.broadcast_to(inv, inv_ref.shape)


def _inv_rows(inv_ref, r, width):
    return pltpu.repeat(inv_ref[r, :], width // LANES, axis=1)


def _pre_rows(x_ref, nw_ref, mod_ref, h_ref, inv_ref, rows_ref):
    _inv_rms_to(x_ref, inv_ref)
    width = x_ref.shape[1]
    rows_ref[0:NORM_ROWS, :] = jnp.broadcast_to(nw_ref[0:1, :] * (1.0 + mod_ref[1, 0]),
                                                (NORM_ROWS, width))
    rows_ref[NORM_ROWS:2 * NORM_ROWS, :] = jnp.broadcast_to(mod_ref[0, 0], (NORM_ROWS, width))

    def body(r):
        y = x_ref[r, :] * _inv_rows(inv_ref, r, width) * rows_ref[0:NORM_ROWS, :]
        h_ref[r, :] = (y + rows_ref[NORM_ROWS:2 * NORM_ROWS, :]).astype(h_ref.dtype)

    _row_loop(x_ref.shape[0], NORM_ROWS, body)


def _post_rows(o_ref, x_ref, src_ref, nw_ref, mod_ref, res_w, inv_ref, rows_ref):
    _inv_rms_to(src_ref, inv_ref)
    width = x_ref.shape[1]
    gate = mod_ref[2, 0] if res_w == 1.0 else res_w * mod_ref[2, 0]
    rows_ref[0:NORM_ROWS, :] = jnp.broadcast_to(gate * nw_ref[1:2, :], (NORM_ROWS, width))

    def body(r):
        y = src_ref[r, :] * _inv_rows(inv_ref, r, width) * rows_ref[0:NORM_ROWS, :]
        o_ref[r, :] = x_ref[r, :] + y

    _row_loop(x_ref.shape[0], NORM_ROWS, body)


def _dot(a, b):
    return jnp.dot(a, b, preferred_element_type=F32)


def _dot_nt(a, b):
    return lax.dot_general(a, b, (((1,), (1,)), ((), ())), preferred_element_type=F32)


def _dot_tn(a, b):
    return lax.dot_general(a, b, (((0,), (0,)), ((), ())), preferred_element_type=F32)


def _split_bf16(v):
    hi = v.astype(BF16)
    lo = (v - hi.astype(F32)).astype(BF16)
    return hi, lo


def _params(semantics, vmem_mib):
    return pltpu.CompilerParams(dimension_semantics=semantics,
                                vmem_limit_bytes=vmem_mib * MIB)


class _Trunk:
    def __init__(self, nseq, seq_len, shared_cond):
        self.nseq = nseq
        self.seq_len = seq_len
        self.tokens = nseq * seq_len
        self.shared_cond = shared_cond

    def cond_row(self, tm):
        if self.shared_cond:
            return lambda i: 0
        per = self.seq_len // tm
        return lambda i: i // per

    def mod_spec(self, tm):
        row = self.cond_row(tm)
        return pl.BlockSpec((3, 1, 1, D), lambda i, *_: (0, row(i), 0, 0))


def _ada_kernel(c_ref, w_ref, b_ref, o_ref):
    sc = _silu(c_ref[...]).astype(BF16)
    o_ref[0] = _dot(sc, w_ref[0].astype(BF16)) + b_ref[0]


def _ada_mod(cond8, ada_w, ada_b):
    tn = 1024
    n = N_MOD * D
    return pl.pallas_call(
        _ada_kernel,
        grid=(DEPTH, n // tn),
        in_specs=[
            pl.BlockSpec((8, D), lambda l, j: (0, 0)),
            pl.BlockSpec((1, D, tn), lambda l, j: (l, 0, j)),
            pl.BlockSpec((1, 1, tn), lambda l, j: (l, 0, j)),
        ],
        out_specs=pl.BlockSpec((1, 8, tn), lambda l, j: (l, 0, j)),
        out_shape=jax.ShapeDtypeStruct((DEPTH, 8, n), F32),
        compiler_params=_params(("arbitrary", "arbitrary"), 40),
    )(cond8, ada_w, ada_b.reshape(DEPTH, 1, n))


def _ffn_kernel(x_ref, mod_ref, nw_ref, wg_ref, wu_ref, wd_ref, o_ref, h_ref, acc_ref, inv_ref,
                rows_ref):
    j = pl.program_id(1)

    @pl.when(j == 0)
    def _():
        _pre_rows(x_ref, nw_ref, mod_ref, h_ref, inv_ref, rows_ref)
        acc_ref[...] = jnp.zeros_like(acc_ref)

    h = h_ref[...]
    g = _dot(h, wg_ref[...])
    u = _dot(h, wu_ref[...])
    a = (_silu(g) * u).astype(BF16)
    acc_ref[...] += _dot(a, wd_ref[...])

    @pl.when(j == pl.num_programs(1) - 1)
    def _():
        _post_rows(o_ref, x_ref, acc_ref, nw_ref, mod_ref, 0.5, inv_ref, rows_ref)


def _ffn(trunk, x, mod3, nw, wg, wu, wd, layer, half):
    tm, tf = 512, 512
    return pl.pallas_call(
        _ffn_kernel,
        grid=(trunk.tokens // tm, D_FF // tf),
        in_specs=[
            pl.BlockSpec((tm, D), lambda i, j: (i, 0)),
            trunk.mod_spec(tm),
            pl.BlockSpec((2, D), lambda i, j: (0, 0)),
            pl.BlockSpec((None, None, D, tf), lambda i, j: (layer, half, 0, j)),
            pl.BlockSpec((None, None, D, tf), lambda i, j: (layer, half, 0, j)),
            pl.BlockSpec((None, None, tf, D), lambda i, j: (layer, half, j, 0)),
        ],
        out_specs=pl.BlockSpec((tm, D), lambda i, j: (i, 0)),
        out_shape=jax.ShapeDtypeStruct((trunk.tokens, D), F32),
        scratch_shapes=[pltpu.VMEM((tm, D), BF16), pltpu.VMEM((tm, D), F32), pltpu.VMEM((tm, LANES), F32), pltpu.VMEM((2 * NORM_ROWS, D), F32)],
        compiler_params=_params(("arbitrary", "arbitrary"), 48),
    )(x, mod3, nw, wg, wu, wd)


def _window_bounds(n, w):
    pos = np.arange(n)
    lo = np.clip(pos - w // 2, 0, n)
    hi = np.clip(pos + (w - w // 2), 0, n)
    return lo, hi


def _band_matrix(n, w):
    lo, hi = _window_bounds(n, w)
    col = np.arange(n)[None, :]
    return ((col >= lo[:, None]) & (col < hi[:, None])).astype(np.float32)


def _window_sum(a, v):
    hi, lo = _split_bf16(v)
    return _dot(a, hi) + _dot(a, lo)


def _inv_rms(x_ref):
    x = x_ref[...]
    return lax.rsqrt(jnp.mean(x * x, axis=-1, keepdims=True) + EPS)


def _pre_slice(x_ref, inv, nw_ref, mod_ref, rows, cols):
    y = x_ref[rows, cols] * inv[rows, :] * nw_ref[0:1, cols]
    return y * (1.0 + mod_ref[1, 0, :, cols]) + mod_ref[0, 0, :, cols]


def _pool_ctx_kernel(x_ref, mod_ref, nw_ref, a_ref, ic_ref, pw_ref, ps_ref, o_ref, mix_ref, inv_ref,
                     rows_ref, *, tm):
    inv = _inv_rms(x_ref)
    gd = POOL_GROUP_DIM
    for g in range(len(POOL_WINDOWS)):
        cols = slice(g * gd, (g + 1) * gd)
        a = a_ref[g]
        for q in range(tm // POOL_CHUNK):
            rows = slice(q * POOL_CHUNK, (q + 1) * POOL_CHUNK)
            v = _pre_slice(x_ref, inv, nw_ref, mod_ref, rows, cols)
            d = (_window_sum(a, v) * ic_ref[g] - v).astype(BF16)
            mix_ref[rows, cols] = _dot(d, pw_ref[g]) * ps_ref[:, cols]
    _post_rows(o_ref, x_ref, mix_ref, nw_ref, mod_ref, 1.0, inv_ref, rows_ref)


def _pool_lat_kernel(xp_ref, xc_ref, xn_ref, mod_ref, nw_ref, a_ref, ic_ref, pw_ref, ps_ref,
                     o_ref, mix_ref, inv_ref, rows_ref, *, blocks_per_seq):
    jb = pl.program_id(0) % blocks_per_seq
    valid_p = jb > 0
    valid_n = jb < blocks_per_seq - 1
    x_refs = (xp_ref, xc_ref, xn_ref)
    invs = tuple(_inv_rms(r) for r in x_refs)
    gd = POOL_GROUP_DIM
    rows_per_chunk = POOL_CHUNK // GRID_W
    rows_per_block = xc_ref.shape[0] // GRID_W
    chunks_per_block = rows_per_block // rows_per_chunk
    for g, w in enumerate(POOL_WINDOWS):
        before, after = w // 2, w - w // 2 - 1
        cols = slice(g * gd, (g + 1) * gd)
        a = a_ref[g]
        first_row = rows_per_block - before
        last_row = 2 * rows_per_block - 1 + after
        cm = {}
        for ch in range(first_row // rows_per_chunk, last_row // rows_per_chunk + 1):
            which = ch // chunks_per_block
            off = (ch % chunks_per_block) * POOL_CHUNK
            v = _pre_slice(x_refs[which], invs[which], nw_ref, mod_ref,
                           slice(off, off + POOL_CHUNK), cols)
            s = _window_sum(a, v)
            if ch < chunks_per_block:
                s = jnp.where(valid_p, s, 0.0)
            elif ch >= 2 * chunks_per_block:
                s = jnp.where(valid_n, s, 0.0)
            cm[ch] = s

        def slab(r):
            o = (r % rows_per_chunk) * GRID_W
            return cm[r // rows_per_chunk][o:o + GRID_W, :]

        outs = []
        for r in range(rows_per_block, 2 * rows_per_block):
            acc = slab(r - before)
            for k in range(-before + 1, after + 1):
                acc = acc + slab(r + k)
            outs.append(acc)
        m = jnp.concatenate(outs, axis=0) * ic_ref[0, g]
        v = _pre_slice(xc_ref, invs[1], nw_ref, mod_ref, slice(None), cols)
        d = (m - v).astype(BF16)
        mix_ref[:, cols] = _dot(d, pw_ref[g]) * ps_ref[:, cols]
    _post_rows(o_ref, xc_ref, mix_ref, nw_ref, mod_ref, 1.0, inv_ref, rows_ref)


def _pool_ctx_consts(seq_len):
    mats, inv = [], []
    for w in POOL_WINDOWS:
        lo, hi = _window_bounds(seq_len, w)
        mats.append(_band_matrix(seq_len, w))
        inv.append((1.0 / (hi - lo)).astype(np.float32)[:, None])
    return jnp.asarray(np.stack(mats), BF16), jnp.asarray(np.stack(inv), F32)


def _pool_lat_consts(rows, tm):
    rows_per_block = tm // GRID_W
    mats, inv = [], []
    for w in POOL_WINDOWS:
        mats.append(np.kron(np.eye(POOL_CHUNK // GRID_W, dtype=np.float32), _band_matrix(GRID_W, w)))
        lo_c, hi_c = _window_bounds(GRID_W, w)
        lo_r, hi_r = _window_bounds(rows, w)
        cnt = (hi_r - lo_r)[:, None] * (hi_c - lo_c)[None, :]
        inv.append((1.0 / cnt).astype(np.float32).reshape(rows // rows_per_block, tm, 1))
    inv = np.stack(inv, axis=1)
    return jnp.asarray(np.stack(mats), BF16), jnp.asarray(inv, F32)


def _pool_mixer(trunk, x, mod3, nw, pool_w, pool_scale, layer, grid_rows):
    tm = 512
    nblk = trunk.tokens // tm
    gd = POOL_GROUP_DIM
    common_specs = [
        trunk.mod_spec(tm),
        pl.BlockSpec((2, D), lambda i: (0, 0)),
        pl.BlockSpec((4, POOL_CHUNK, POOL_CHUNK), lambda i: (0, 0, 0)),
    ]
    tail_specs = [
        pl.BlockSpec((None, 4, gd, gd), lambda i: (layer, 0, 0, 0)),
        pl.BlockSpec((None, 1, D), lambda i: (layer, 0, 0)),
    ]
    out_spec = pl.BlockSpec((tm, D), lambda i: (i, 0))
    out_shape = jax.ShapeDtypeStruct((trunk.tokens, D), F32)
    scratch = [pltpu.VMEM((tm, D), F32), pltpu.VMEM((tm, LANES), F32), pltpu.VMEM((2 * NORM_ROWS, D), F32)]
    if grid_rows is None:
        assert trunk.seq_len == POOL_CHUNK
        a, ic = _pool_ctx_consts(trunk.seq_len)
        return pl.pallas_call(
            functools.partial(_pool_ctx_kernel, tm=tm),
            grid=(nblk,),
            in_specs=[pl.BlockSpec((tm, D), lambda i: (i, 0))] + common_specs
            + [pl.BlockSpec((4, POOL_CHUNK, 1), lambda i: (0, 0, 0))] + tail_specs,
            out_specs=out_spec, out_shape=out_shape, scratch_shapes=scratch,
            compiler_params=_params(("arbitrary",), 48),
        )(x, mod3, nw, a, ic, pool_w, pool_scale)
    blocks_per_seq = trunk.seq_len // tm
    assert max(POOL_WINDOWS) // 2 <= tm // GRID_W
    a, ic = _pool_lat_consts(grid_rows, tm)
    return pl.pallas_call(
        functools.partial(_pool_lat_kernel, blocks_per_seq=blocks_per_seq),
        grid=(nblk,),
        in_specs=[
            pl.BlockSpec((tm, D), lambda i: (jnp.maximum(i - 1, 0), 0)),
            pl.BlockSpec((tm, D), lambda i: (i, 0)),
            pl.BlockSpec((tm, D), lambda i: (jnp.minimum(i + 1, nblk - 1), 0)),
        ] + common_specs
        + [pl.BlockSpec((1, 4, tm, 1), lambda i: (i % blocks_per_seq, 0, 0, 0))] + tail_specs,
        out_specs=out_spec, out_shape=out_shape, scratch_shapes=scratch,
        compiler_params=_params(("arbitrary",), 56),
    )(x, x, x, mod3, nw, a, ic, pool_w, pool_scale)


def _ssd_in_kernel(x_ref, mod_ref, nw_ref, w_ref, wdt_ref, zx_ref, dt_ref, h_ref, inv_ref, rows_ref):
    @pl.when(pl.program_id(1) == 0)
    def _():
        _pre_rows(x_ref, nw_ref, mod_ref, h_ref, inv_ref, rows_ref)
        dt_ref[...] = _dot(h_ref[...], wdt_ref[...])

    zx_ref[...] = _dot(h_ref[...], w_ref[...])


def _ssd_in(trunk, x, mod3, nw, in_w, layer):
    tm, tn = 512, 1024
    ndt = 2 * SSD_HEADS
    return pl.pallas_call(
        _ssd_in_kernel,
        grid=(trunk.tokens // tm, SSD_ZX_DIM // tn),
        in_specs=[
            pl.BlockSpec((tm, D), lambda i, j: (i, 0)),
            trunk.mod_spec(tm),
            pl.BlockSpec((2, D), lambda i, j: (0, 0)),
            pl.BlockSpec((None, D, tn), lambda i, j: (layer, 0, j)),
            pl.BlockSpec((None, D, ndt), lambda i, j: (layer, 0, SSD_ZX_DIM // ndt)),
        ],
        out_specs=[
            pl.BlockSpec((tm, tn), lambda i, j: (i, j)),
            pl.BlockSpec((tm, ndt), lambda i, j: (i, 0)),
        ],
        out_shape=[
            jax.ShapeDtypeStruct((trunk.tokens, SSD_ZX_DIM), F32),
            jax.ShapeDtypeStruct((trunk.tokens, ndt), F32),
        ],
        scratch_shapes=[pltpu.VMEM((tm, D), BF16), pltpu.VMEM((tm, LANES), F32), pltpu.VMEM((2 * NORM_ROWS, D), F32)],
        compiler_params=_params(("arbitrary", "arbitrary"), 40),
    )(x, mod3, nw, in_w, in_w)


def _ssd_conv_strips(buf_ref, w_ref, b_ref, first, count, store):
    base = SSD_HALO - SSD_CONV // 2
    rb = 4 * SUBLANES

    def strip(s, carry):
        for r in range(0, SSD_CHUNK, rb):
            acc = b_ref[s][None]
            for t in range(SSD_CONV):
                x = buf_ref[s, base + t + r:base + t + r + rb, :]
                acc = acc + w_ref[s, t][None] * x.reshape(rb // SUBLANES, SUBLANES, SSD_STRIP)
            store(s, r, _silu(acc).reshape(rb, SSD_STRIP))
        return carry

    lax.fori_loop(first, first + count, strip, 0)


def _ssd_scan_kernel(*refs, reverse, nc, has_h0, emit_state, alias_state):
    it = iter(refs)
    if not reverse:
        raw = [(next(it), next(it), next(it)) for _ in range(3)]
        cw_ref, cbias_ref = next(it), next(it)
    else:
        xsc_ref, bcc_ref = next(it), next(it)
    dt_ref, dtb_ref, alog_ref, e_ref = next(it), next(it), next(it), next(it)
    h0_ref = next(it) if has_h0 else None
    if reverse:
        yf_ref, z_ref, gn_ref = next(it), next(it), next(it)
    else:
        dskip_ref = next(it)
    if alias_state:
        next(it)
    if reverse:
        yn_ref = next(it)
    else:
        yf_ref, xsc_ref, bcc_ref = next(it), next(it), next(it)
    so_ref = next(it) if emit_state else None
    state_ref, acol_s, arow_s, dtrow_s, lhs_off_s, lhs_w_s = (next(it) for _ in range(6))
    if reverse:
        y_s, ytmp_ref, inv_ref = next(it), next(it), next(it)
    else:
        xbuf_ref, bct_ref = next(it), next(it)

    t = SSD_CHUNK
    step = pl.program_id(1)
    cc = (nc - 1 - step) if reverse else step
    gdim = SSD_GROUP_DIM

    @pl.when(step == 0)
    def _():
        if has_h0:
            for g in range(SSD_GROUPS):
                for k in range(gdim // t):
                    r0 = g * gdim + k * t
                    state_ref[g, :, k * t:(k + 1) * t] = h0_ref[0, 0, r0:r0 + t, :].T
        else:
            state_ref[...] = jnp.zeros_like(state_ref)

    if not reverse:
        has_prev = cc > 0
        has_next = cc < nc - 1
        s0 = 0
        for cur, prev, nxt in raw:
            for k in range(cur.shape[1] // SSD_STRIP):
                lanes = slice(k * SSD_STRIP, (k + 1) * SSD_STRIP)
                xbuf_ref[s0 + k, 0:SSD_HALO, :] = jnp.where(has_prev, prev[:, lanes], 0.0)
                xbuf_ref[s0 + k, SSD_HALO:SSD_HALO + t, :] = cur[:, lanes]
                xbuf_ref[s0 + k, SSD_HALO + t:SSD_HALO + t + SSD_HALO, :] = jnp.where(
                    has_next, nxt[:, lanes], 0.0)
            s0 += cur.shape[1] // SSD_STRIP

        def store_x(s, r, v):
            xsc_ref[0, s, r:r + v.shape[0], :] = v

        def store_bc(s, r, v):
            bct_ref[s - SSD_X_STRIPS, r:r + v.shape[0], :] = v

        _ssd_conv_strips(xbuf_ref, cw_ref, cbias_ref, 0, SSD_X_STRIPS, store_x)
        _ssd_conv_strips(xbuf_ref, cw_ref, cbias_ref, SSD_X_STRIPS, SSD_BC_STRIPS, store_bc)
        per = SSD_STRIP // SSD_STATE
        for k in range(2 * SSD_GROUPS):
            bcc_ref[0, k] = bct_ref[k // per, :, (k % per) * SSD_STATE:(k % per + 1) * SSD_STATE].astype(BF16)

    dt_raw = dt_ref[...] + dtb_ref[...]
    dt = jnp.maximum(dt_raw, 0.0) + jnp.log1p(jnp.exp(-jnp.abs(dt_raw)))
    dta = dt * (-jnp.exp(alog_ref[...]))

    row = lax.broadcasted_iota(jnp.int32, (t, t), 0)
    col = lax.broadcasted_iota(jnp.int32, (t, t), 1)
    mask = (col >= row) if reverse else (col <= row)
    acum = jnp.dot(mask.astype(F32), dta, precision=lax.Precision.HIGHEST,
                   preferred_element_type=F32)
    acum_t = acum.T
    dt_t = dt.T
    end = 0 if reverse else t - 1
    d0 = SSD_HEADS if reverse else 0
    mine = (col >= d0) & (col < d0 + SSD_HEADS)

    hi, lo = _split_bf16(jnp.where(mine, jnp.exp(acum), 0.0))
    lhs_off_s[...] = jnp.concatenate([hi, lo], axis=1)
    hi, lo = _split_bf16(jnp.where(mine, dt * jnp.exp(acum[end:end + 1, :] - acum), 0.0))
    lhs_w_s[...] = jnp.concatenate([hi, lo], axis=1)

    hpg = SSD_HEADS_PER_GROUP
    for g in range(SSD_GROUPS):
        heads = slice(d0 + g * hpg, d0 + (g + 1) * hpg)
        acol_s[g] = acum[:, heads]
        arow_s[g] = acum_t[heads, :]
        dtrow_s[g] = dt_t[heads, :]

    pair = 2 * SSD_HEAD_DIM
    first_head = lax.broadcasted_iota(jnp.int32, (t, pair), 1) < SSD_HEAD_DIM

    def group_body(g, carry):
        xg = xsc_ref[0, g]
        bg = bcc_ref[0, g]
        cg = bcc_ref[0, SSD_GROUPS + g]
        acol = acol_s[g]
        arow = arow_s[g]
        dtrow = dtrow_s[g]
        off = _dot(lhs_off_s[...], e_ref[g])
        wgt = _dot(lhs_w_s[...], e_ref[g])
        cb = _dot_nt(cg, bg)
        st = state_ref[g]
        y_off = _dot(cg, st.astype(BF16))
        new = _dot_tn(bg, (xg * wgt).astype(BF16))
        state_ref[g] = st * off[end:end + 1, :] + new
        ys = []
        for p in range(hpg // 2):
            lanes = slice(p * pair, (p + 1) * pair)
            sc = []
            for r in (2 * p, 2 * p + 1):
                lm = jnp.exp(jnp.where(mask, acol[:, r:r + 1] - arow[r:r + 1, :], -jnp.inf))
                sc.append((cb * lm * dtrow[r:r + 1, :]).astype(BF16))
            xp = xg[:, lanes]
            x_bd = jnp.concatenate([jnp.where(first_head, xp, 0.0),
                                    jnp.where(first_head, 0.0, xp)], axis=0).astype(BF16)
            ys.append(_dot(jnp.concatenate(sc, axis=1), x_bd) + y_off[:, lanes] * off[:, lanes])
        y = jnp.concatenate(ys, axis=1)
        if reverse:
            y_s[g] = y
        else:
            yf_ref[0, g] = y + dskip_ref[g] * xg
        return carry

    lax.fori_loop(0, SSD_GROUPS, group_body, 0, unroll=4)

    if reverse:
        for g in range(SSD_GROUPS):
            lanes = slice(g * gdim, (g + 1) * gdim)
            ytmp_ref[:, lanes] = (yf_ref[0, g] + y_s[g]) * _silu(z_ref[:, lanes])
        _inv_rms_to(ytmp_ref, inv_ref)

        def norm(r):
            y = ytmp_ref[r, :] * _inv_rows(inv_ref, r, SSD_D_INNER) * gn_ref[...]
            yn_ref[r, :] = y.astype(yn_ref.dtype)

        _row_loop(t, NORM_ROWS, norm)

    if emit_state:
        @pl.when(step == nc - 1)
        def _():
            for g in range(SSD_GROUPS):
                for k in range(gdim // t):
                    r0 = g * gdim + k * t
                    so_ref[0, 0, 0, r0:r0 + t, :] = state_ref[g, :, k * t:(k + 1) * t].T


def _ssd_expand_consts():
    e = np.zeros((2, SSD_GROUPS, 4 * SSD_HEADS, SSD_GROUP_DIM), np.float32)
    for d in range(2):
        for h in range(SSD_HEADS):
            g, r = divmod(h, SSD_HEADS_PER_GROUP)
            for half in range(2):
                e[d, g, half * 2 * SSD_HEADS + d * SSD_HEADS + h,
                  r * SSD_HEAD_DIM:(r + 1) * SSD_HEAD_DIM] = 1.0
    return jnp.asarray(e, BF16)


def _ssd_scan(trunk, p, layer, dt, *, reverse, zx=None, conv=None, y_fwd=None, h0=None,
              state_buf=None):
    t = SSD_CHUNK
    nc = trunk.seq_len // t
    nchunks = trunk.tokens // t
    emit_state = h0 is None
    alias_state = emit_state and state_buf is not None
    hb = t // SSD_HALO
    last8 = trunk.tokens // SSD_HALO - 1
    di, bc = SSD_D_INNER, SSD_BC_DIM
    n_strips = SSD_X_STRIPS + SSD_BC_STRIPS
    ndt = 2 * SSD_HEADS
    direction = 1 if reverse else 0

    def blk(b, c):
        return b * nc + ((nc - 1 - c) if reverse else c)

    def full(shape):
        return pl.BlockSpec(shape, lambda b, c: (0,) * len(shape))

    def chunk_major(lead, rows, lanes):
        return pl.BlockSpec((1, lead, rows, lanes), lambda b, c: (blk(b, c), 0, 0, 0))

    inputs, specs = [], []

    def add(arr, spec):
        inputs.append(arr)
        specs.append(spec)

    if not reverse:
        for width, colb in ((di, 1), (bc, 2 * di // bc), (bc, 2 * di // bc + 1)):
            add(zx, pl.BlockSpec((t, width), lambda b, c, colb=colb: (blk(b, c), colb)))
            add(zx, pl.BlockSpec((SSD_HALO, width),
                                 lambda b, c, colb=colb: (jnp.maximum(blk(b, c) * hb - 1, 0), colb)))
            add(zx, pl.BlockSpec((SSD_HALO, width),
                                 lambda b, c, colb=colb: (jnp.minimum(blk(b, c) * hb + hb, last8), colb)))
        add(p["conv_w"], pl.BlockSpec((None, n_strips, SSD_CONV, SUBLANES, SSD_STRIP),
                                      lambda b, c: (layer, 0, 0, 0, 0)))
        add(p["conv_b"], pl.BlockSpec((None, n_strips, SUBLANES, SSD_STRIP),
                                      lambda b, c: (layer, 0, 0, 0)))
    else:
        add(conv[0], chunk_major(SSD_GROUPS, t, SSD_GROUP_DIM))
        add(conv[1], chunk_major(2 * SSD_GROUPS, t, SSD_STATE))
    add(dt, pl.BlockSpec((t, ndt), lambda b, c: (blk(b, c), 0)))
    add(p["dt_bias"], pl.BlockSpec((None, 1, ndt), lambda b, c: (layer, 0, 0)))
    add(p["a_log"], pl.BlockSpec((None, 1, ndt), lambda b, c: (layer, 0, 0)))
    add(_ssd_expand_consts(), pl.BlockSpec((None, SSD_GROUPS, 2 * ndt, SSD_GROUP_DIM),
                                           lambda b, c: (direction, 0, 0, 0)))
    if h0 is not None:
        add(h0, pl.BlockSpec((1, 1, di, SSD_STATE), lambda b, c: (b, direction, 0, 0)))
    if reverse:
        add(y_fwd, chunk_major(SSD_GROUPS, t, SSD_GROUP_DIM))
        add(zx, pl.BlockSpec((t, di), lambda b, c: (blk(b, c), 0)))
        add(p["gnorm"], pl.BlockSpec((None, 1, di), lambda b, c: (layer, 0, 0)))
    else:
        add(p["d_skip"], pl.BlockSpec((None, SSD_GROUPS, 1, SSD_GROUP_DIM), lambda b, c: (layer, 0, 0, 0)))
    aliases = {}
    if alias_state:
        aliases[len(inputs)] = 1 if reverse else 3
        add(state_buf, pl.BlockSpec(memory_space=pl.ANY))

    if reverse:
        out_specs = [pl.BlockSpec((t, di), lambda b, c: (blk(b, c), 0))]
        out_shape = [jax.ShapeDtypeStruct((trunk.tokens, di), BF16)]
    else:
        out_specs = [chunk_major(SSD_GROUPS, t, SSD_GROUP_DIM),
                     chunk_major(SSD_GROUPS, t, SSD_GROUP_DIM),
                     chunk_major(2 * SSD_GROUPS, t, SSD_STATE)]
        out_shape = [jax.ShapeDtypeStruct((nchunks, SSD_GROUPS, t, SSD_GROUP_DIM), F32),
                     jax.ShapeDtypeStruct((nchunks, SSD_GROUPS, t, SSD_GROUP_DIM), F32),
                     jax.ShapeDtypeStruct((nchunks, 2 * SSD_GROUPS, t, SSD_STATE), BF16)]
    if emit_state:
        n_layers = p["a_log"].shape[0]
        out_specs.append(pl.BlockSpec((1, 1, 1, di, SSD_STATE),
                                      lambda b, c: (b, layer, direction, 0, 0)))
        out_shape.append(jax.ShapeDtypeStruct((trunk.nseq, n_layers, 2, di, SSD_STATE), F32))

    hpg = SSD_HEADS_PER_GROUP
    scratch = [
        pltpu.VMEM((SSD_GROUPS, SSD_STATE, SSD_GROUP_DIM), F32),
        pltpu.VMEM((SSD_GROUPS, t, hpg), F32),
        pltpu.VMEM((SSD_GROUPS, hpg, t), F32),
        pltpu.VMEM((SSD_GROUPS, hpg, t), F32),
        pltpu.VMEM((t, 2 * ndt), BF16),
        pltpu.VMEM((t, 2 * ndt), BF16),
    ]
    if reverse:
        scratch += [pltpu.VMEM((SSD_GROUPS, t, SSD_GROUP_DIM), F32), pltpu.VMEM((t, di), F32),
                    pltpu.VMEM((t, LANES), F32)]
    else:
        scratch += [pltpu.VMEM((n_strips, t + 2 * SSD_HALO, SSD_STRIP), F32),
                    pltpu.VMEM((SSD_BC_STRIPS, t, SSD_STRIP), F32)]
    res = pl.pallas_call(
        functools.partial(_ssd_scan_kernel, reverse=reverse, nc=nc, has_h0=h0 is not None,
                          emit_state=emit_state, alias_state=alias_state),
        grid=(trunk.nseq, nc),
        in_specs=specs, out_specs=out_specs, out_shape=out_shape, scratch_shapes=scratch,
        input_output_aliases=aliases,
        compiler_params=_params(("arbitrary", "arbitrary"), 48),
    )(*inputs)
    return res


def _mm_post_kernel(a_ref, w_ref, x_ref, mod_ref, nw_ref, o_ref, acc_ref, inv_ref, rows_ref):
    k = pl.program_id(1)

    @pl.when(k == 0)
    def _():
        acc_ref[...] = jnp.zeros_like(acc_ref)

    acc_ref[...] += _dot(a_ref[...], w_ref[...])

    @pl.when(k == pl.num_programs(1) - 1)
    def _():
        _post_rows(o_ref, x_ref, acc_ref, nw_ref, mod_ref, 1.0, inv_ref, rows_ref)


def _mm_post(trunk, a, w, layer, x, mod3, nw):
    tm, tk = 512, 1024
    kdim = a.shape[1]
    return pl.pallas_call(
        _mm_post_kernel,
        grid=(trunk.tokens // tm, kdim // tk),
        in_specs=[
            pl.BlockSpec((tm, tk), lambda i, k: (i, k)),
            pl.BlockSpec((None, tk, D), lambda i, k: (layer, k, 0)),
            pl.BlockSpec((tm, D), lambda i, k: (i, 0)),
            trunk.mod_spec(tm),
            pl.BlockSpec((2, D), lambda i, k: (0, 0)),
        ],
        out_specs=pl.BlockSpec((tm, D), lambda i, k: (i, 0)),
        out_shape=jax.ShapeDtypeStruct((trunk.tokens, D), F32),
        scratch_shapes=[pltpu.VMEM((tm, D), F32), pltpu.VMEM((tm, LANES), F32), pltpu.VMEM((2 * NORM_ROWS, D), F32)],
        compiler_params=_params(("arbitrary", "arbitrary"), 40),
    )(a, w, x, mod3, nw)


def _ssd_mixer(trunk, x, mod3, nw, h0, p, layer, state_buf):
    zx, dt = _ssd_in(trunk, x, mod3, nw, p["in_w"], layer)
    res = _ssd_scan(trunk, p, layer, dt, reverse=False, zx=zx, h0=h0, state_buf=state_buf)
    y_f, conv = res[0], (res[1], res[2])
    if h0 is None:
        state_buf = res[3]
    res = _ssd_scan(trunk, p, layer, dt, reverse=True, zx=zx, conv=conv, y_fwd=y_f, h0=h0,
                    state_buf=state_buf)
    if h0 is None:
        state_buf = res[1]
    return _mm_post(trunk, res[0], p["out_w"], layer, x, mod3, nw), state_buf


def _cm_in_kernel(x_ref, mod_ref, nw_ref, wa_ref, wb_ref, o_ref, h_ref, inv_ref, rows_ref):
    @pl.when(pl.program_id(1) == 0)
    def _():
        _pre_rows(x_ref, nw_ref, mod_ref, h_ref, inv_ref, rows_ref)

    h = h_ref[...]
    o_ref[...] = _dot(h, wa_ref[...]) * _sigmoid(_dot(h, wb_ref[...]))


def _cm_in(trunk, x, mod3, nw, pw1, layer):
    tm, tn = 512, 512
    nb = D // tn
    return pl.pallas_call(
        _cm_in_kernel,
        grid=(trunk.tokens // tm, nb),
        in_specs=[
            pl.BlockSpec((tm, D), lambda i, j: (i, 0)),
            trunk.mod_spec(tm),
            pl.BlockSpec((2, D), lambda i, j: (0, 0)),
            pl.BlockSpec((None, D, tn), lambda i, j: (layer, 0, j)),
            pl.BlockSpec((None, D, tn), lambda i, j: (layer, 0, j + nb)),
        ],
        out_specs=pl.BlockSpec((tm, tn), lambda i, j: (i, j)),
        out_shape=jax.ShapeDtypeStruct((trunk.tokens, D), F32),
        scratch_shapes=[pltpu.VMEM((tm, D), BF16), pltpu.VMEM((tm, LANES), F32), pltpu.VMEM((2 * NORM_ROWS, D), F32)],
        compiler_params=_params(("arbitrary", "arbitrary"), 40),
    )(x, mod3, nw, pw1, pw1)


def _cm_out_kernel(u_ref, up_ref, un_ref, dw_ref, db_ref, lw_ref, lb_ref, w2_ref, x_ref, mod_ref,
                   nw_ref, o_ref, buf_ref, sh_ref, cv_ref, v_ref, out_ref, inv_ref, rows_ref, *,
                   blocks_per_seq):
    tm = u_ref.shape[0]
    jb = pl.program_id(0) % blocks_per_seq
    n_strips = D // CM_STRIP
    for s in range(n_strips):
        lanes = slice(s * CM_STRIP, (s + 1) * CM_STRIP)
        buf_ref[s, 0:CM_HALO, :] = jnp.where(jb > 0, up_ref[:, lanes], 0.0)
        buf_ref[s, CM_HALO:CM_HALO + tm, :] = u_ref[:, lanes]
        buf_ref[s, CM_HALO + tm:CM_HALO + tm + CM_HALO, :] = jnp.where(
            jb < blocks_per_seq - 1, un_ref[:, lanes], 0.0)

    base = CM_HALO - CM_KERNEL // 2
    span = tm + (base + CM_KERNEL - 1) // SUBLANES * SUBLANES
    rb = 4 * SUBLANES

    def strip(s, carry):
        for r in range(1, SUBLANES):
            sh_ref[r - 1] = buf_ref[s, r:r + span, :]
        for r0 in range(0, tm, rb):
            acc = db_ref[s][None]
            for t in range(CM_KERNEL):
                q, r = divmod(base + t, SUBLANES)
                lo = r0 + q * SUBLANES
                x = buf_ref[s, lo:lo + rb, :] if r == 0 else sh_ref[r - 1, lo:lo + rb, :]
                acc = acc + dw_ref[s, t][None] * x.reshape(rb // SUBLANES, SUBLANES, CM_STRIP)
            cv_ref[s, r0:r0 + rb, :] = acc.reshape(rb, CM_STRIP)
        return carry

    lax.fori_loop(0, n_strips, strip, 0)

    total = cv_ref[0]
    for s in range(1, n_strips):
        total = total + cv_ref[s]
    mu = jnp.sum(total, axis=-1, keepdims=True) * (1.0 / D)
    sq = None
    for s in range(n_strips):
        cen = cv_ref[s] - mu
        sq = cen * cen if sq is None else sq + cen * cen
    inv = lax.rsqrt(jnp.sum(sq, axis=-1, keepdims=True) * (1.0 / D) + EPS)
    for s in range(n_strips):
        lanes = slice(s * CM_STRIP, (s + 1) * CM_STRIP)
        y = (cv_ref[s] - mu) * inv * lw_ref[:, lanes] + lb_ref[:, lanes]
        v_ref[:, lanes] = _silu(y).astype(BF16)
    out_ref[...] = _dot(v_ref[...], w2_ref[...])
    _post_rows(o_ref, x_ref, out_ref, nw_ref, mod_ref, 1.0, inv_ref, rows_ref)


def _cm_out(trunk, u, c, layer, x, mod3, nw):
    tm = 256
    nblk = trunk.tokens // tm
    blocks_per_seq = trunk.seq_len // tm
    hb = tm // CM_HALO
    last = trunk.tokens // CM_HALO - 1
    n_strips = D // CM_STRIP
    span = tm + (CM_HALO - CM_KERNEL // 2 + CM_KERNEL - 1) // SUBLANES * SUBLANES
    row = pl.BlockSpec((None, 1, D), lambda i: (layer, 0, 0))
    return pl.pallas_call(
        functools.partial(_cm_out_kernel, blocks_per_seq=blocks_per_seq),
        grid=(nblk,),
        in_specs=[
            pl.BlockSpec((tm, D), lambda i: (i, 0)),
            pl.BlockSpec((CM_HALO, D), lambda i: (jnp.maximum(i * hb - 1, 0), 0)),
            pl.BlockSpec((CM_HALO, D), lambda i: (jnp.minimum(i * hb + hb, last), 0)),
            pl.BlockSpec((None, n_strips, CM_KERNEL, SUBLANES, CM_STRIP), lambda i: (layer, 0, 0, 0, 0)),
            pl.BlockSpec((None, n_strips, SUBLANES, CM_STRIP), lambda i: (layer, 0, 0, 0)),
            row, row,
            pl.BlockSpec((None, D, D), lambda i: (layer, 0, 0)),
            pl.BlockSpec((tm, D), lambda i: (i, 0)),
            trunk.mod_spec(tm),
            pl.BlockSpec((2, D), lambda i: (0, 0)),
        ],
        out_specs=pl.BlockSpec((tm, D), lambda i: (i, 0)),
        out_shape=jax.ShapeDtypeStruct((trunk.tokens, D), F32),
        scratch_shapes=[pltpu.VMEM((n_strips, tm + 2 * CM_HALO, CM_STRIP), F32),
                        pltpu.VMEM((SUBLANES - 1, span, CM_STRIP), F32),
                        pltpu.VMEM((n_strips, tm, CM_STRIP), F32),
                        pltpu.VMEM((tm, D), BF16),
                        pltpu.VMEM((tm, D), F32),
                        pltpu.VMEM((tm, LANES), F32), pltpu.VMEM((2 * NORM_ROWS, D), F32)],
        compiler_params=_params(("arbitrary",), 48),
    )(u, u, u, c["dw_w"], c["dw_b"], c["ln_w"], c["ln_b"], c["pw2"], x, mod3, nw)


def _strip_rows(v, strip):
    v = v.reshape(v.shape[:-1] + (v.shape[-1] // strip, 1, strip))
    return jnp.broadcast_to(v, v.shape[:-2] + (SUBLANES, strip))


def _run_trunk(trunk, x, mod_t, h0_all, grid_rows, w):
    state_buf = None
    for i in range(DEPTH):
        kind, j = i % 3, i // 3
        x = _ffn(trunk, x, mod_t[i, 0:3], w["norm_w"][i, 0], w["wg"], w["wu"], w["wd"], i, 0)
        mod3, nw = mod_t[i, 3:6], w["norm_w"][i, 1]
        if kind == 0:
            x = _pool_mixer(trunk, x, mod3, nw, w["pool_w"], w["pool_scale"], j, grid_rows)
        elif kind == 1:
            h0 = None if h0_all is None else h0_all[:, j].reshape(trunk.nseq, 2, SSD_D_INNER, SSD_STATE)
            x, state_buf = _ssd_mixer(trunk, x, mod3, nw, h0, w["ssd"], j, state_buf)
        else:
            u = _cm_in(trunk, x, mod3, nw, w["cm"]["pw1"], j)
            x = _cm_out(trunk, u, w["cm"], j, x, mod3, nw)
        x = _ffn(trunk, x, mod_t[i, 6:9], w["norm_w"][i, 2], w["wg"], w["wu"], w["wd"], i, 1)
    return x, state_buf


def kernel(x_prompt, x_sample, state_ssd, c, c_ctx, norm_w, ada_w, ada_b, ffn_wg, ffn_wu, ffn_wd, pool_w, pool_scale, ssd_in_w, ssd_conv_w, ssd_conv_b, ssd_a_log, ssd_dt_bias, ssd_d, ssd_norm_w, ssd_out_w, cm_pw1, cm_dw_w, cm_dw_b, cm_ln_w, cm_ln_b, cm_pw2):
    n_ctx, l_ctx, _ = x_prompt.shape
    n_lat, l_lat, _ = x_sample.shape
    ctx = _Trunk(n_ctx, l_ctx, shared_cond=True)
    lat = _Trunk(n_lat, l_lat, shared_cond=False)

    cond8 = jnp.zeros((8, D), F32).at[0].set(c_ctx).at[1:1 + n_lat].set(c)
    mod = _ada_mod(cond8, ada_w, ada_b).reshape(DEPTH, 8, N_MOD, D)
    mod = jnp.transpose(mod, (0, 2, 1, 3))[:, :, :, None, :]
    mod_ctx, mod_lat = mod[:, :, 0:1], mod[:, :, 1:1 + n_lat]

    n_ssd = ssd_in_w.shape[0]
    n_cm = cm_pw1.shape[0]
    ndt = 2 * SSD_HEADS
    w = {
        "norm_w": norm_w,
        "wg": ffn_wg.astype(BF16), "wu": ffn_wu.astype(BF16), "wd": ffn_wd.astype(BF16),
        "pool_w": pool_w.astype(BF16),
        "pool_scale": pool_scale[:, None, :],
        "ssd": dict(
            in_w=ssd_in_w.astype(BF16),
            conv_w=_strip_rows(ssd_conv_w, SSD_STRIP).transpose(0, 2, 1, 3, 4),
            conv_b=_strip_rows(ssd_conv_b, SSD_STRIP),
            dt_bias=ssd_dt_bias.reshape(n_ssd, 1, ndt), a_log=ssd_a_log.reshape(n_ssd, 1, ndt),
            d_skip=jnp.repeat(ssd_d, SSD_HEAD_DIM, axis=-1).reshape(n_ssd, SSD_GROUPS, 1, SSD_GROUP_DIM),
            gnorm=ssd_norm_w[:, None, :], out_w=ssd_out_w.astype(BF16)),
        "cm": dict(
            pw1=cm_pw1.astype(BF16),
            dw_w=_strip_rows(cm_dw_w, CM_STRIP).transpose(0, 2, 1, 3, 4),
            dw_b=_strip_rows(cm_dw_b, CM_STRIP),
            ln_w=cm_ln_w[:, None, :], ln_b=cm_ln_b[:, None, :], pw2=cm_pw2.astype(BF16)),
    }

    y_ctx, new_state = _run_trunk(ctx, x_prompt.reshape(ctx.tokens, D), mod_ctx, None, None, w)
    y_lat, _ = _run_trunk(lat, x_sample.reshape(lat.tokens, D), mod_lat, state_ssd,
                          l_lat // GRID_W, w)

    new_state = new_state.reshape(n_ctx, n_ssd, 2, SSD_HEADS, SSD_HEAD_DIM, SSD_STATE)
    return (y_ctx.reshape(x_prompt.shape), y_lat.reshape(x_sample.shape),
            new_state.astype(x_prompt.dtype))
```

```python
import functools

import numpy as np
import jax
import jax.numpy as jnp
from jax import lax
from jax.experimental import pallas as pl
from jax.experimental.pallas import tpu as pltpu

F32 = jnp.float32
BF16 = jnp.bfloat16

D = 2048
DEPTH = 4
N_MOD = 9
D_FF = 5632
EPS = 1e-6
GRID_W = 64
POOL_WINDOWS = (2, 4, 8, 16)
POOL_GROUP_DIM = D // 4
SSD_D_INNER = 2 * D
SSD_HEAD_DIM = 64
SSD_HEADS = SSD_D_INNER // SSD_HEAD_DIM
SSD_GROUPS = 8
SSD_HEADS_PER_GROUP = SSD_HEADS // SSD_GROUPS
SSD_GROUP_DIM = SSD_HEADS_PER_GROUP * SSD_HEAD_DIM
SSD_STATE = 128
SSD_CONV = 5
SSD_CHUNK = 128
SSD_BC_DIM = SSD_GROUPS * SSD_STATE
SSD_ZX_DIM = 2 * SSD_D_INNER + 2 * SSD_BC_DIM
SSD_HALO = 8
SSD_STRIP = SSD_GROUP_DIM
SSD_X_STRIPS = SSD_D_INNER // SSD_STRIP
SSD_BC_STRIPS = 2 * SSD_BC_DIM // SSD_STRIP
CM_KERNEL = 31
CM_HALO = 16
CM_STRIP = 512
POOL_CHUNK = 256
SUBLANES = 8
LANES = 128
NORM_ROWS = 16

MIB = 1024 * 1024


def _silu(x):
    return x * (1.0 / (1.0 + jnp.exp(-x)))


def _sigmoid(x):
    return 1.0 / (1.0 + jnp.exp(-x))


def _rms(x, w):
    return x * lax.rsqrt(jnp.mean(x * x, axis=-1, keepdims=True) + EPS) * w


def _row_loop(n_rows, rows, body):
    def step(i, carry):
        body(pl.ds(pl.multiple_of(i * rows, rows), rows))
        return carry

    lax.fori_loop(0, n_rows // rows, step, 0, unroll=2)


def _inv_rms_to(src_ref, inv_ref):
    x = src_ref[...]
    inv = lax.rsqrt(jnp.mean(x * x, axis=-1, keepdims=True) + EPS)
    inv_ref[...] = jnp.broadcast_to(inv, inv_ref.shape)


def _inv_rows(inv_ref, r, width):
    return jnp.concatenate([inv_ref[r, :]] * (width // LANES), axis=1)


def _pre_rows(x_ref, nw_ref, mod_ref, h_ref, inv_ref, rows_ref):
    _inv_rms_to(x_ref, inv_ref)
    width = x_ref.shape[1]
    rows_ref[0:NORM_ROWS, :] = jnp.broadcast_to(nw_ref[0:1, :] * (1.0 + mod_ref[1, 0]),
                                                (NORM_ROWS, width))
    rows_ref[NORM_ROWS:2 * NORM_ROWS, :] = jnp.broadcast_to(mod_ref[0, 0], (NORM_ROWS, width))

    def body(r):
        y = x_ref[r, :] * _inv_rows(inv_ref, r, width) * rows_ref[0:NORM_ROWS, :]
        h_ref[r, :] = (y + rows_ref[NORM_ROWS:2 * NORM_ROWS, :]).astype(h_ref.dtype)

    _row_loop(x_ref.shape[0], NORM_ROWS, body)


def _post_rows(o_ref, x_ref, src_ref, nw_ref, mod_ref, res_w, inv_ref, rows_ref):
    _inv_rms_to(src_ref, inv_ref)
    width = x_ref.shape[1]
    gate = mod_ref[2, 0] if res_w == 1.0 else res_w * mod_ref[2, 0]
    rows_ref[0:NORM_ROWS, :] = jnp.broadcast_to(gate * nw_ref[1:2, :], (NORM_ROWS, width))

    def body(r):
        y = src_ref[r, :] * _inv_rows(inv_ref, r, width) * rows_ref[0:NORM_ROWS, :]
        o_ref[r, :] = x_ref[r, :] + y

    _row_loop(x_ref.shape[0], NORM_ROWS, body)


def _dot(a, b):
    return jnp.dot(a, b, preferred_element_type=F32)


def _dot_nt(a, b):
    return lax.dot_general(a, b, (((1,), (1,)), ((), ())), preferred_element_type=F32)


def _dot_tn(a, b):
    return lax.dot_general(a, b, (((0,), (0,)), ((), ())), preferred_element_type=F32)


def _split_bf16(v):
    hi = v.astype(BF16)
    lo = (v - hi.astype(F32)).astype(BF16)
    return hi, lo


def _params(semantics, vmem_mib):
    return pltpu.CompilerParams(dimension_semantics=semantics,
                                vmem_limit_bytes=vmem_mib * MIB)


class _Trunk:
    def __init__(self, nseq, seq_len, shared_cond):
        self.nseq = nseq
        self.seq_len = seq_len
        self.tokens = nseq * seq_len
        self.shared_cond = shared_cond

    def cond_row(self, tm):
        if self.shared_cond:
            return lambda i: 0
        per = self.seq_len // tm
        return lambda i: i // per

    def mod_spec(self, tm):
        row = self.cond_row(tm)
        return pl.BlockSpec((3, 1, 1, D), lambda i, *_: (0, row(i), 0, 0))


def _ada_kernel(c_ref, w_ref, b_ref, o_ref):
    sc = _silu(c_ref[...]).astype(BF16)
    o_ref[0] = _dot(sc, w_ref[0].astype(BF16)) + b_ref[0]


def _ada_mod(cond8, ada_w, ada_b):
    tn = 1024
    n = N_MOD * D
    return pl.pallas_call(
        _ada_kernel,
        grid=(DEPTH, n // tn),
        in_specs=[
            pl.BlockSpec((8, D), lambda l, j: (0, 0)),
            pl.BlockSpec((1, D, tn), lambda l, j: (l, 0, j)),
            pl.BlockSpec((1, 1, tn), lambda l, j: (l, 0, j)),
        ],
        out_specs=pl.BlockSpec((1, 8, tn), lambda l, j: (l, 0, j)),
        out_shape=jax.ShapeDtypeStruct((DEPTH, 8, n), F32),
        compiler_params=_params(("arbitrary", "arbitrary"), 52),
    )(cond8, ada_w, ada_b.reshape(DEPTH, 1, n))


def _ffn_kernel(x_ref, mod_ref, nw_ref, wg_ref, wu_ref, wd_ref, o_ref, h_ref, inv_ref, rows_ref):
    j = pl.program_id(1)

    @pl.when(j == 0)
    def _():
        _pre_rows(x_ref, nw_ref, mod_ref, h_ref, inv_ref, rows_ref)
        o_ref[...] = jnp.zeros_like(o_ref)

    h = h_ref[...]
    g = _dot(h, wg_ref[...])
    u = _dot(h, wu_ref[...])
    a = (_silu(g) * u).astype(BF16)
    o_ref[...] += _dot(a, wd_ref[...])

    @pl.when(j == pl.num_programs(1) - 1)
    def _():
        _post_rows(o_ref, x_ref, o_ref, nw_ref, mod_ref, 0.5, inv_ref, rows_ref)


def _ffn(trunk, x, mod3, nw, wg, wu, wd, layer, half):
    tm, tf = 1024, 512
    return pl.pallas_call(
        _ffn_kernel,
        grid=(trunk.tokens // tm, D_FF // tf),
        in_specs=[
            pl.BlockSpec((tm, D), lambda i, j: (i, 0)),
            trunk.mod_spec(tm),
            pl.BlockSpec((2, D), lambda i, j: (0, 0)),
            pl.BlockSpec((None, None, D, tf), lambda i, j: (layer, half, 0, j)),
            pl.BlockSpec((None, None, D, tf), lambda i, j: (layer, half, 0, j)),
            pl.BlockSpec((None, None, tf, D), lambda i, j: (layer, half, j, 0)),
        ],
        out_specs=pl.BlockSpec((tm, D), lambda i, j: (i, 0)),
        out_shape=jax.ShapeDtypeStruct((trunk.tokens, D), F32),
        scratch_shapes=[pltpu.VMEM((tm, D), BF16), pltpu.VMEM((tm, LANES), F32),
                        pltpu.VMEM((2 * NORM_ROWS, D), F32)],
        compiler_params=_params(("arbitrary", "arbitrary"), 58),
    )(x, mod3, nw, wg, wu, wd)


def _window_bounds(n, w):
    pos = np.arange(n)
    lo = np.clip(pos - w // 2, 0, n)
    hi = np.clip(pos + (w - w // 2), 0, n)
    return lo, hi


def _band_matrix(n, w):
    lo, hi = _window_bounds(n, w)
    col = np.arange(n)[None, :]
    return ((col >= lo[:, None]) & (col < hi[:, None])).astype(np.float32)


def _window_sum(a, v):
    hi, lo = _split_bf16(v)
    return _dot(a, hi) + _dot(a, lo)


def _inv_rms(x_ref):
    x = x_ref[...]
    return lax.rsqrt(jnp.mean(x * x, axis=-1, keepdims=True) + EPS)


def _pre_slice(x_ref, inv, nw_ref, mod_ref, rows, cols):
    y = x_ref[rows, cols] * inv[rows, :] * nw_ref[0:1, cols]
    return y * (1.0 + mod_ref[1, 0, :, cols]) + mod_ref[0, 0, :, cols]


def _pool_ctx_kernel(x_ref, mod_ref, nw_ref, a_ref, ic_ref, pw_ref, ps_ref, o_ref, mix_ref, inv_ref,
                     rows_ref, *, tm):
    inv = _inv_rms(x_ref)
    gd = POOL_GROUP_DIM
    for g in range(len(POOL_WINDOWS)):
        cols = slice(g * gd, (g + 1) * gd)
        a = a_ref[g]
        for q in range(tm // POOL_CHUNK):
            rows = slice(q * POOL_CHUNK, (q + 1) * POOL_CHUNK)
            v = _pre_slice(x_ref, inv, nw_ref, mod_ref, rows, cols)
            d = (_window_sum(a, v) * ic_ref[g] - v).astype(BF16)
            mix_ref[rows, cols] = _dot(d, pw_ref[g]) * ps_ref[:, cols]
    _post_rows(o_ref, x_ref, mix_ref, nw_ref, mod_ref, 1.0, inv_ref, rows_ref)


def _pool_lat_kernel(xp_ref, xc_ref, xn_ref, mod_ref, nw_ref, a_ref, ic_ref, pw_ref, ps_ref,
                     o_ref, mix_ref, inv_ref, rows_ref, *, blocks_per_seq):
    jb = pl.program_id(0) % blocks_per_seq
    valid_p = jb > 0
    valid_n = jb < blocks_per_seq - 1
    x_refs = (xp_ref, xc_ref, xn_ref)
    invs = tuple(_inv_rms(r) for r in x_refs)
    gd = POOL_GROUP_DIM
    rows_per_chunk = POOL_CHUNK // GRID_W
    rows_per_block = xc_ref.shape[0] // GRID_W
    chunks_per_block = rows_per_block // rows_per_chunk
    for g, w in enumerate(POOL_WINDOWS):
        before, after = w // 2, w - w // 2 - 1
        cols = slice(g * gd, (g + 1) * gd)
        a = a_ref[g]
        first_row = rows_per_block - before
        last_row = 2 * rows_per_block - 1 + after
        cm = {}
        for ch in range(first_row // rows_per_chunk, last_row // rows_per_chunk + 1):
            which = ch // chunks_per_block
            off = (ch % chunks_per_block) * POOL_CHUNK
            v = _pre_slice(x_refs[which], invs[which], nw_ref, mod_ref,
                           slice(off, off + POOL_CHUNK), cols)
            s = _window_sum(a, v)
            if ch < chunks_per_block:
                s = jnp.where(valid_p, s, 0.0)
            elif ch >= 2 * chunks_per_block:
                s = jnp.where(valid_n, s, 0.0)
            cm[ch] = s

        def slab(r):
            o = (r % rows_per_chunk) * GRID_W
            return cm[r // rows_per_chunk][o:o + GRID_W, :]

        outs = []
        for r in range(rows_per_block, 2 * rows_per_block):
            acc = slab(r - before)
            for k in range(-before + 1, after + 1):
                acc = acc + slab(r + k)
            outs.append(acc)
        m = jnp.concatenate(outs, axis=0) * ic_ref[0, g]
        v = _pre_slice(xc_ref, invs[1], nw_ref, mod_ref, slice(None), cols)
        d = (m - v).astype(BF16)
        mix_ref[:, cols] = _dot(d, pw_ref[g]) * ps_ref[:, cols]
    _post_rows(o_ref, xc_ref, mix_ref, nw_ref, mod_ref, 1.0, inv_ref, rows_ref)


def _pool_ctx_consts(seq_len):
    mats, inv = [], []
    for w in POOL_WINDOWS:
        lo, hi = _window_bounds(seq_len, w)
        mats.append(_band_matrix(seq_len, w))
        inv.append((1.0 / (hi - lo)).astype(np.float32)[:, None])
    return jnp.asarray(np.stack(mats), BF16), jnp.asarray(np.stack(inv), F32)


def _pool_lat_consts(rows, tm):
    rows_per_block = tm // GRID_W
    mats, inv = [], []
    for w in POOL_WINDOWS:
        mats.append(np.kron(np.eye(POOL_CHUNK // GRID_W, dtype=np.float32), _band_matrix(GRID_W, w)))
        lo_c, hi_c = _window_bounds(GRID_W, w)
        lo_r, hi_r = _window_bounds(rows, w)
        cnt = (hi_r - lo_r)[:, None] * (hi_c - lo_c)[None, :]
        inv.append((1.0 / cnt).astype(np.float32).reshape(rows // rows_per_block, tm, 1))
    inv = np.stack(inv, axis=1)
    return jnp.asarray(np.stack(mats), BF16), jnp.asarray(inv, F32)


def _pool_mixer(trunk, x, mod3, nw, pool_w, pool_scale, layer, grid_rows):
    tm = 512
    nblk = trunk.tokens // tm
    gd = POOL_GROUP_DIM
    common_specs = [
        trunk.mod_spec(tm),
        pl.BlockSpec((2, D), lambda i: (0, 0)),
        pl.BlockSpec((4, POOL_CHUNK, POOL_CHUNK), lambda i: (0, 0, 0)),
    ]
    tail_specs = [
        pl.BlockSpec((None, 4, gd, gd), lambda i: (layer, 0, 0, 0)),
        pl.BlockSpec((None, 1, D), lambda i: (layer, 0, 0)),
    ]
    out_spec = pl.BlockSpec((tm, D), lambda i: (i, 0))
    out_shape = jax.ShapeDtypeStruct((trunk.tokens, D), F32)
    scratch = [pltpu.VMEM((tm, D), F32), pltpu.VMEM((tm, LANES), F32), pltpu.VMEM((2 * NORM_ROWS, D), F32)]
    if grid_rows is None:
        assert trunk.seq_len == POOL_CHUNK
        a, ic = _pool_ctx_consts(trunk.seq_len)
        return pl.pallas_call(
            functools.partial(_pool_ctx_kernel, tm=tm),
            grid=(nblk,),
            in_specs=[pl.BlockSpec((tm, D), lambda i: (i, 0))] + common_specs
            + [pl.BlockSpec((4, POOL_CHUNK, 1), lambda i: (0, 0, 0))] + tail_specs,
            out_specs=out_spec, out_shape=out_shape, scratch_shapes=scratch,
            compiler_params=_params(("arbitrary",), 48),
        )(x, mod3, nw, a, ic, pool_w, pool_scale)
    blocks_per_seq = trunk.seq_len // tm
    assert max(POOL_WINDOWS) // 2 <= tm // GRID_W
    a, ic = _pool_lat_consts(grid_rows, tm)
    return pl.pallas_call(
        functools.partial(_pool_lat_kernel, blocks_per_seq=blocks_per_seq),
        grid=(nblk,),
        in_specs=[
            pl.BlockSpec((tm, D), lambda i: (jnp.maximum(i - 1, 0), 0)),
            pl.BlockSpec((tm, D), lambda i: (i, 0)),
            pl.BlockSpec((tm, D), lambda i: (jnp.minimum(i + 1, nblk - 1), 0)),
        ] + common_specs
        + [pl.BlockSpec((1, 4, tm, 1), lambda i: (i % blocks_per_seq, 0, 0, 0))] + tail_specs,
        out_specs=out_spec, out_shape=out_shape, scratch_shapes=scratch,
        compiler_params=_params(("arbitrary",), 56),
    )(x, x, x, mod3, nw, a, ic, pool_w, pool_scale)


def _ssd_in_kernel(x_ref, mod_ref, nw_ref, w_ref, wdt_ref, zx_ref, dt_ref, h_ref, inv_ref, rows_ref):
    @pl.when(pl.program_id(1) == 0)
    def _():
        _pre_rows(x_ref, nw_ref, mod_ref, h_ref, inv_ref, rows_ref)
        dt_ref[...] = _dot(h_ref[...], wdt_ref[...])

    zx_ref[...] = _dot(h_ref[...], w_ref[...])


def _ssd_in(trunk, x, mod3, nw, in_w, layer):
    tm, tn = 512, 2048
    ndt = 2 * SSD_HEADS
    return pl.pallas_call(
        _ssd_in_kernel,
        grid=(trunk.tokens // tm, SSD_ZX_DIM // tn),
        in_specs=[
            pl.BlockSpec((tm, D), lambda i, j: (i, 0)),
            trunk.mod_spec(tm),
            pl.BlockSpec((2, D), lambda i, j: (0, 0)),
            pl.BlockSpec((None, D, tn), lambda i, j: (layer, 0, j)),
            pl.BlockSpec((None, D, ndt), lambda i, j: (layer, 0, SSD_ZX_DIM // ndt)),
        ],
        out_specs=[
            pl.BlockSpec((tm, tn), lambda i, j: (i, j)),
            pl.BlockSpec((tm, ndt), lambda i, j: (i, 0)),
        ],
        out_shape=[
            jax.ShapeDtypeStruct((trunk.tokens, SSD_ZX_DIM), F32),
            jax.ShapeDtypeStruct((trunk.tokens, ndt), F32),
        ],
        scratch_shapes=[pltpu.VMEM((tm, D), BF16), pltpu.VMEM((tm, LANES), F32), pltpu.VMEM((2 * NORM_ROWS, D), F32)],
        compiler_params=_params(("arbitrary", "arbitrary"), 52),
    )(x, mod3, nw, in_w, in_w)


def _ssd_conv_strips(buf_ref, w_ref, b_ref, first, count, store):
    base = SSD_HALO - SSD_CONV // 2
    rb = 4 * SUBLANES

    def strip(s, carry):
        for r in range(0, SSD_CHUNK, rb):
            acc = b_ref[s][None]
            for t in range(SSD_CONV):
                x = buf_ref[s, base + t + r:base + t + r + rb, :]
                acc = acc + w_ref[s, t][None] * x.reshape(rb // SUBLANES, SUBLANES, SSD_STRIP)
            store(s, r, _silu(acc).reshape(rb, SSD_STRIP))
        return carry

    lax.fori_loop(first, first + count, strip, 0)


def _ssd_scan_kernel(*refs, reverse, nc, has_h0, emit_state, alias_state):
    it = iter(refs)
    if not reverse:
        raw = [(next(it), next(it), next(it)) for _ in range(3)]
        cw_ref, cbias_ref = next(it), next(it)
    else:
        xsc_ref, bcc_ref = next(it), next(it)
    dt_ref, dtb_ref, alog_ref, e_ref = next(it), next(it), next(it), next(it)
    h0_ref = next(it) if has_h0 else None
    if reverse:
        yf_ref, z_ref, gn_ref = next(it), next(it), next(it)
    else:
        dskip_ref = next(it)
    if alias_state:
        next(it)
    if reverse:
        yn_ref = next(it)
    else:
        yf_ref, xsc_ref, bcc_ref = next(it), next(it), next(it)
    so_ref = next(it) if emit_state else None
    state_ref, acol_s, arow_s, dtrow_s, lhs_off_s, lhs_w_s = (next(it) for _ in range(6))
    if reverse:
        y_s, ytmp_ref, inv_ref = next(it), next(it), next(it)
    else:
        xbuf_ref, bct_ref = next(it), next(it)

    t = SSD_CHUNK
    step = pl.program_id(1)
    cc = (nc - 1 - step) if reverse else step
    gdim = SSD_GROUP_DIM

    @pl.when(step == 0)
    def _():
        if has_h0:
            for g in range(SSD_GROUPS):
                for k in range(gdim // t):
                    r0 = g * gdim + k * t
                    state_ref[g, :, k * t:(k + 1) * t] = h0_ref[0, 0, r0:r0 + t, :].T
        else:
            state_ref[...] = jnp.zeros_like(state_ref)

    if not reverse:
        has_prev = cc > 0
        has_next = cc < nc - 1
        s0 = 0
        for cur, prev, nxt in raw:
            for k in range(cur.shape[1] // SSD_STRIP):
                lanes = slice(k * SSD_STRIP, (k + 1) * SSD_STRIP)
                xbuf_ref[s0 + k, 0:SSD_HALO, :] = jnp.where(has_prev, prev[:, lanes], 0.0)
                xbuf_ref[s0 + k, SSD_HALO:SSD_HALO + t, :] = cur[:, lanes]
                xbuf_ref[s0 + k, SSD_HALO + t:SSD_HALO + t + SSD_HALO, :] = jnp.where(
                    has_next, nxt[:, lanes], 0.0)
            s0 += cur.shape[1] // SSD_STRIP

        def store_x(s, r, v):
            xsc_ref[0, s, r:r + v.shape[0], :] = v

        def store_bc(s, r, v):
            bct_ref[s - SSD_X_STRIPS, r:r + v.shape[0], :] = v

        _ssd_conv_strips(xbuf_ref, cw_ref, cbias_ref, 0, SSD_X_STRIPS, store_x)
        _ssd_conv_strips(xbuf_ref, cw_ref, cbias_ref, SSD_X_STRIPS, SSD_BC_STRIPS, store_bc)
        per = SSD_STRIP // SSD_STATE
        for k in range(2 * SSD_GROUPS):
            bcc_ref[0, k] = bct_ref[k // per, :, (k % per) * SSD_STATE:(k % per + 1) * SSD_STATE].astype(BF16)

    dt_raw = dt_ref[...] + dtb_ref[...]
    dt = jnp.maximum(dt_raw, 0.0) + jnp.log1p(jnp.exp(-jnp.abs(dt_raw)))
    dta = dt * (-jnp.exp(alog_ref[...]))

    row = lax.broadcasted_iota(jnp.int32, (t, t), 0)
    col = lax.broadcasted_iota(jnp.int32, (t, t), 1)
    mask = (col >= row) if reverse else (col <= row)
    acum = jnp.dot(mask.astype(F32), dta, precision=lax.Precision.HIGHEST,
                   preferred_element_type=F32)
    acum_t = acum.T
    dt_t = dt.T
    end = 0 if reverse else t - 1
    d0 = SSD_HEADS if reverse else 0
    mine = (col >= d0) & (col < d0 + SSD_HEADS)

    hi, lo = _split_bf16(jnp.where(mine, jnp.exp(acum), 0.0))
    lhs_off_s[...] = jnp.concatenate([hi, lo], axis=1)
    hi, lo = _split_bf16(jnp.where(mine, dt * jnp.exp(acum[end:end + 1, :] - acum), 0.0))
    lhs_w_s[...] = jnp.concatenate([hi, lo], axis=1)

    hpg = SSD_HEADS_PER_GROUP
    for g in range(SSD_GROUPS):
        heads = slice(d0 + g * hpg, d0 + (g + 1) * hpg)
        acol_s[g] = acum[:, heads]
        arow_s[g] = acum_t[heads, :]
        dtrow_s[g] = dt_t[heads, :]

    pair = 2 * SSD_HEAD_DIM
    first_head = lax.broadcasted_iota(jnp.int32, (t, pair), 1) < SSD_HEAD_DIM

    def group_body(g, carry):
        xg = xsc_ref[0, g]
        bg = bcc_ref[0, g]
        cg = bcc_ref[0, SSD_GROUPS + g]
        acol = acol_s[g]
        arow = arow_s[g]
        dtrow = dtrow_s[g]
        off = _dot(lhs_off_s[...], e_ref[g])
        wgt = _dot(lhs_w_s[...], e_ref[g])
        cb = _dot_nt(cg, bg)
        st = state_ref[g]
        y_off = _dot(cg, st.astype(BF16))
        new = _dot_tn(bg, (xg * wgt).astype(BF16))
        state_ref[g] = st * off[end:end + 1, :] + new
        ys = []
        for p in range(hpg // 2):
            lanes = slice(p * pair, (p + 1) * pair)
            sc = []
            for r in (2 * p, 2 * p + 1):
                lm = jnp.exp(jnp.where(mask, acol[:, r:r + 1] - arow[r:r + 1, :], -jnp.inf))
                sc.append((cb * lm * dtrow[r:r + 1, :]).astype(BF16))
            xp = xg[:, lanes]
            x_bd = jnp.concatenate([jnp.where(first_head, xp, 0.0),
                                    jnp.where(first_head, 0.0, xp)], axis=0).astype(BF16)
            ys.append(_dot(jnp.concatenate(sc, axis=1), x_bd) + y_off[:, lanes] * off[:, lanes])
        y = jnp.concatenate(ys, axis=1)
        if reverse:
            y_s[g] = y
        else:
            yf_ref[0, g] = y + dskip_ref[g] * xg
        return carry

    lax.fori_loop(0, SSD_GROUPS, group_body, 0, unroll=4)

    if reverse:
        for g in range(SSD_GROUPS):
            lanes = slice(g * gdim, (g + 1) * gdim)
            ytmp_ref[:, lanes] = (yf_ref[0, g] + y_s[g]) * _silu(z_ref[:, lanes])
        _inv_rms_to(ytmp_ref, inv_ref)

        def norm(r):
            y = ytmp_ref[r, :] * _inv_rows(inv_ref, r, SSD_D_INNER) * gn_ref[...]
            yn_ref[r, :] = y.astype(yn_ref.dtype)

        _row_loop(t, NORM_ROWS, norm)

    if emit_state:
        @pl.when(step == nc - 1)
        def _():
            for g in range(SSD_GROUPS):
                for k in range(gdim // t):
                    r0 = g * gdim + k * t
                    so_ref[0, 0, 0, r0:r0 + t, :] = state_ref[g, :, k * t:(k + 1) * t].T


def _ssd_expand_consts():
    e = np.zeros((2, SSD_GROUPS, 4 * SSD_HEADS, SSD_GROUP_DIM), np.float32)
    for d in range(2):
        for h in range(SSD_HEADS):
            g, r = divmod(h, SSD_HEADS_PER_GROUP)
            for half in range(2):
                e[d, g, half * 2 * SSD_HEADS + d * SSD_HEADS + h,
                  r * SSD_HEAD_DIM:(r + 1) * SSD_HEAD_DIM] = 1.0
    return jnp.asarray(e, BF16)


def _ssd_scan(trunk, p, layer, dt, *, reverse, zx=None, conv=None, y_fwd=None, h0=None,
              state_buf=None):
    t = SSD_CHUNK
    nc = trunk.seq_len // t
    nchunks = trunk.tokens // t
    emit_state = h0 is None
    alias_state = emit_state and state_buf is not None
    hb = t // SSD_HALO
    last8 = trunk.tokens // SSD_HALO - 1
    di, bc = SSD_D_INNER, SSD_BC_DIM
    n_strips = SSD_X_STRIPS + SSD_BC_STRIPS
    ndt = 2 * SSD_HEADS
    direction = 1 if reverse else 0

    def blk(b, c):
        return b * nc + ((nc - 1 - c) if reverse else c)

    def full(shape):
        return pl.BlockSpec(shape, lambda b, c: (0,) * len(shape))

    def chunk_major(lead, rows, lanes):
        return pl.BlockSpec((1, lead, rows, lanes), lambda b, c: (blk(b, c), 0, 0, 0))

    inputs, specs = [], []

    def add(arr, spec):
        inputs.append(arr)
        specs.append(spec)

    if not reverse:
        for width, colb in ((di, 1), (bc, 2 * di // bc), (bc, 2 * di // bc + 1)):
            add(zx, pl.BlockSpec((t, width), lambda b, c, colb=colb: (blk(b, c), colb)))
            add(zx, pl.BlockSpec((SSD_HALO, width),
                                 lambda b, c, colb=colb: (jnp.maximum(blk(b, c) * hb - 1, 0), colb)))
            add(zx, pl.BlockSpec((SSD_HALO, width),
                                 lambda b, c, colb=colb: (jnp.minimum(blk(b, c) * hb + hb, last8), colb)))
        add(p["conv_w"], pl.BlockSpec((None, n_strips, SSD_CONV, SUBLANES, SSD_STRIP),
                                      lambda b, c: (layer, 0, 0, 0, 0)))
        add(p["conv_b"], pl.BlockSpec((None, n_strips, SUBLANES, SSD_STRIP),
                                      lambda b, c: (layer, 0, 0, 0)))
    else:
        add(conv[0], chunk_major(SSD_GROUPS, t, SSD_GROUP_DIM))
        add(conv[1], chunk_major(2 * SSD_GROUPS, t, SSD_STATE))
    add(dt, pl.BlockSpec((t, ndt), lambda b, c: (blk(b, c), 0)))
    add(p["dt_bias"], pl.BlockSpec((None, 1, ndt), lambda b, c: (layer, 0, 0)))
    add(p["a_log"], pl.BlockSpec((None, 1, ndt), lambda b, c: (layer, 0, 0)))
    add(_ssd_expand_consts(), pl.BlockSpec((None, SSD_GROUPS, 2 * ndt, SSD_GROUP_DIM),
                                           lambda b, c: (direction, 0, 0, 0)))
    if h0 is not None:
        add(h0, pl.BlockSpec((1, 1, di, SSD_STATE), lambda b, c: (b, direction, 0, 0)))
    if reverse:
        add(y_fwd, chunk_major(SSD_GROUPS, t, SSD_GROUP_DIM))
        add(zx, pl.BlockSpec((t, di), lambda b, c: (blk(b, c), 0)))
        add(p["gnorm"], pl.BlockSpec((None, 1, di), lambda b, c: (layer, 0, 0)))
    else:
        add(p["d_skip"], pl.BlockSpec((None, SSD_GROUPS, 1, SSD_GROUP_DIM), lambda b, c: (layer, 0, 0, 0)))
    aliases = {}
    if alias_state:
        aliases[len(inputs)] = 1 if reverse else 3
        add(state_buf, pl.BlockSpec(memory_space=pl.ANY))

    if reverse:
        out_specs = [pl.BlockSpec((t, di), lambda b, c: (blk(b, c), 0))]
        out_shape = [jax.ShapeDtypeStruct((trunk.tokens, di), BF16)]
    else:
        out_specs = [chunk_major(SSD_GROUPS, t, SSD_GROUP_DIM),
                     chunk_major(SSD_GROUPS, t, SSD_GROUP_DIM),
                     chunk_major(2 * SSD_GROUPS, t, SSD_STATE)]
        out_shape = [jax.ShapeDtypeStruct((nchunks, SSD_GROUPS, t, SSD_GROUP_DIM), F32),
                     jax.ShapeDtypeStruct((nchunks, SSD_GROUPS, t, SSD_GROUP_DIM), F32),
                     jax.ShapeDtypeStruct((nchunks, 2 * SSD_GROUPS, t, SSD_STATE), BF16)]
    if emit_state:
        n_layers = p["a_log"].shape[0]
        out_specs.append(pl.BlockSpec((1, 1, 1, di, SSD_STATE),
                                      lambda b, c: (b, layer, direction, 0, 0)))
        out_shape.append(jax.ShapeDtypeStruct((trunk.nseq, n_layers, 2, di, SSD_STATE), F32))

    hpg = SSD_HEADS_PER_GROUP
    scratch = [
        pltpu.VMEM((SSD_GROUPS, SSD_STATE, SSD_GROUP_DIM), F32),
        pltpu.VMEM((SSD_GROUPS, t, hpg), F32),
        pltpu.VMEM((SSD_GROUPS, hpg, t), F32),
        pltpu.VMEM((SSD_GROUPS, hpg, t), F32),
        pltpu.VMEM((t, 2 * ndt), BF16),
        pltpu.VMEM((t, 2 * ndt), BF16),
    ]
    if reverse:
        scratch += [pltpu.VMEM((SSD_GROUPS, t, SSD_GROUP_DIM), F32), pltpu.VMEM((t, di), F32),
                    pltpu.VMEM((t, LANES), F32)]
    else:
        scratch += [pltpu.VMEM((n_strips, t + 2 * SSD_HALO, SSD_STRIP), F32),
                    pltpu.VMEM((SSD_BC_STRIPS, t, SSD_STRIP), F32)]
    res = pl.pallas_call(
        functools.partial(_ssd_scan_kernel, reverse=reverse, nc=nc, has_h0=h0 is not None,
                          emit_state=emit_state, alias_state=alias_state),
        grid=(trunk.nseq, nc),
        in_specs=specs, out_specs=out_specs, out_shape=out_shape, scratch_shapes=scratch,
        input_output_aliases=aliases,
        compiler_params=_params(("arbitrary", "arbitrary"), 48),
    )(*inputs)
    return res


def _mm_post_kernel(a_ref, w_ref, x_ref, mod_ref, nw_ref, o_ref, acc_ref, inv_ref, rows_ref):
    k = pl.program_id(1)

    @pl.when(k == 0)
    def _():
        acc_ref[...] = jnp.zeros_like(acc_ref)

    acc_ref[...] += _dot(a_ref[...], w_ref[...])

    @pl.when(k == pl.num_programs(1) - 1)
    def _():
        _post_rows(o_ref, x_ref, acc_ref, nw_ref, mod_ref, 1.0, inv_ref, rows_ref)


def _mm_post(trunk, a, w, layer, x, mod3, nw):
    tm, tk = 512, 2048
    kdim = a.shape[1]
    return pl.pallas_call(
        _mm_post_kernel,
        grid=(trunk.tokens // tm, kdim // tk),
        in_specs=[
            pl.BlockSpec((tm, tk), lambda i, k: (i, k)),
            pl.BlockSpec((None, tk, D), lambda i, k: (layer, k, 0)),
            pl.BlockSpec((tm, D), lambda i, k: (i, 0)),
            trunk.mod_spec(tm),
            pl.BlockSpec((2, D), lambda i, k: (0, 0)),
        ],
        out_specs=pl.BlockSpec((tm, D), lambda i, k: (i, 0)),
        out_shape=jax.ShapeDtypeStruct((trunk.tokens, D), F32),
        scratch_shapes=[pltpu.VMEM((tm, D), F32), pltpu.VMEM((tm, LANES), F32), pltpu.VMEM((2 * NORM_ROWS, D), F32)],
        compiler_params=_params(("arbitrary", "arbitrary"), 52),
    )(a, w, x, mod3, nw)


def _ssd_mixer(trunk, x, mod3, nw, h0, p, layer, state_buf):
    zx, dt = _ssd_in(trunk, x, mod3, nw, p["in_w"], layer)
    res = _ssd_scan(trunk, p, layer, dt, reverse=False, zx=zx, h0=h0, state_buf=state_buf)
    y_f, conv = res[0], (res[1], res[2])
    if h0 is None:
        state_buf = res[3]
    res = _ssd_scan(trunk, p, layer, dt, reverse=True, zx=zx, conv=conv, y_fwd=y_f, h0=h0,
                    state_buf=state_buf)
    if h0 is None:
        state_buf = res[1]
    return _mm_post(trunk, res[0], p["out_w"], layer, x, mod3, nw), state_buf


def _cm_in_kernel(x_ref, mod_ref, nw_ref, wa_ref, wb_ref, o_ref, h_ref, inv_ref, rows_ref):
    @pl.when(pl.program_id(1) == 0)
    def _():
        _pre_rows(x_ref, nw_ref, mod_ref, h_ref, inv_ref, rows_ref)

    h = h_ref[...]
    o_ref[...] = _dot(h, wa_ref[...]) * _sigmoid(_dot(h, wb_ref[...]))


def _cm_in(trunk, x, mod3, nw, pw1, layer):
    tm, tn = 512, 1024
    nb = D // tn
    return pl.pallas_call(
        _cm_in_kernel,
        grid=(trunk.tokens // tm, nb),
        in_specs=[
            pl.BlockSpec((tm, D), lambda i, j: (i, 0)),
            trunk.mod_spec(tm),
            pl.BlockSpec((2, D), lambda i, j: (0, 0)),
            pl.BlockSpec((None, D, tn), lambda i, j: (layer, 0, j)),
            pl.BlockSpec((None, D, tn), lambda i, j: (layer, 0, j + nb)),
        ],
        out_specs=pl.BlockSpec((tm, tn), lambda i, j: (i, j)),
        out_shape=jax.ShapeDtypeStruct((trunk.tokens, D), F32),
        scratch_shapes=[pltpu.VMEM((tm, D), BF16), pltpu.VMEM((tm, LANES), F32), pltpu.VMEM((2 * NORM_ROWS, D), F32)],
        compiler_params=_params(("arbitrary", "arbitrary"), 52),
    )(x, mod3, nw, pw1, pw1)


def _cm_out_kernel(u_ref, up_ref, un_ref, dw_ref, db_ref, lw_ref, lb_ref, w2_ref, x_ref, mod_ref,
                   nw_ref, o_ref, buf_ref, sh_ref, cv_ref, v_ref, out_ref, inv_ref, rows_ref, *,
                   blocks_per_seq):
    tm = u_ref.shape[0]
    jb = pl.program_id(0) % blocks_per_seq
    n_strips = D // CM_STRIP
    for s in range(n_strips):
        lanes = slice(s * CM_STRIP, (s + 1) * CM_STRIP)
        buf_ref[s, 0:CM_HALO, :] = jnp.where(jb > 0, up_ref[:, lanes], 0.0)
        buf_ref[s, CM_HALO:CM_HALO + tm, :] = u_ref[:, lanes]
        buf_ref[s, CM_HALO + tm:CM_HALO + tm + CM_HALO, :] = jnp.where(
            jb < blocks_per_seq - 1, un_ref[:, lanes], 0.0)

    base = CM_HALO - CM_KERNEL // 2
    span = tm + (base + CM_KERNEL - 1) // SUBLANES * SUBLANES
    rb = 4 * SUBLANES

    def strip(s, carry):
        for r in range(1, SUBLANES):
            sh_ref[r - 1] = buf_ref[s, r:r + span, :]
        for r0 in range(0, tm, rb):
            acc = db_ref[s][None]
            for t in range(CM_KERNEL):
                q, r = divmod(base + t, SUBLANES)
                lo = r0 + q * SUBLANES
                x = buf_ref[s, lo:lo + rb, :] if r == 0 else sh_ref[r - 1, lo:lo + rb, :]
                acc = acc + dw_ref[s, t][None] * x.reshape(rb // SUBLANES, SUBLANES, CM_STRIP)
            cv_ref[s, r0:r0 + rb, :] = acc.reshape(rb, CM_STRIP)
        return carry

    lax.fori_loop(0, n_strips, strip, 0)

    total = cv_ref[0]
    for s in range(1, n_strips):
        total = total + cv_ref[s]
    mu = jnp.sum(total, axis=-1, keepdims=True) * (1.0 / D)
    sq = None
    for s in range(n_strips):
        cen = cv_ref[s] - mu
        sq = cen * cen if sq is None else sq + cen * cen
    inv = lax.rsqrt(jnp.sum(sq, axis=-1, keepdims=True) * (1.0 / D) + EPS)
    for s in range(n_strips):
        lanes = slice(s * CM_STRIP, (s + 1) * CM_STRIP)
        y = (cv_ref[s] - mu) * inv * lw_ref[:, lanes] + lb_ref[:, lanes]
        v_ref[:, lanes] = _silu(y).astype(BF16)
    out_ref[...] = _dot(v_ref[...], w2_ref[...])
    _post_rows(o_ref, x_ref, out_ref, nw_ref, mod_ref, 1.0, inv_ref, rows_ref)


def _cm_out(trunk, u, c, layer, x, mod3, nw):
    tm = 256
    nblk = trunk.tokens // tm
    blocks_per_seq = trunk.seq_len // tm
    hb = tm // CM_HALO
    last = trunk.tokens // CM_HALO - 1
    n_strips = D // CM_STRIP
    span = tm + (CM_HALO - CM_KERNEL // 2 + CM_KERNEL - 1) // SUBLANES * SUBLANES
    row = pl.BlockSpec((None, 1, D), lambda i: (layer, 0, 0))
    return pl.pallas_call(
        functools.partial(_cm_out_kernel, blocks_per_seq=blocks_per_seq),
        grid=(nblk,),
        in_specs=[
            pl.BlockSpec((tm, D), lambda i: (i, 0)),
            pl.BlockSpec((CM_HALO, D), lambda i: (jnp.maximum(i * hb - 1, 0), 0)),
            pl.BlockSpec((CM_HALO, D), lambda i: (jnp.minimum(i * hb + hb, last), 0)),
            pl.BlockSpec((None, n_strips, CM_KERNEL, SUBLANES, CM_STRIP), lambda i: (layer, 0, 0, 0, 0)),
            pl.BlockSpec((None, n_strips, SUBLANES, CM_STRIP), lambda i: (layer, 0, 0, 0)),
            row, row,
            pl.BlockSpec((None, D, D), lambda i: (layer, 0, 0)),
            pl.BlockSpec((tm, D), lambda i: (i, 0)),
            trunk.mod_spec(tm),
            pl.BlockSpec((2, D), lambda i: (0, 0)),
        ],
        out_specs=pl.BlockSpec((tm, D), lambda i: (i, 0)),
        out_shape=jax.ShapeDtypeStruct((trunk.tokens, D), F32),
        scratch_shapes=[pltpu.VMEM((n_strips, tm + 2 * CM_HALO, CM_STRIP), F32),
                        pltpu.VMEM((SUBLANES - 1, span, CM_STRIP), F32),
                        pltpu.VMEM((n_strips, tm, CM_STRIP), F32),
                        pltpu.VMEM((tm, D), BF16),
                        pltpu.VMEM((tm, D), F32),
                        pltpu.VMEM((tm, LANES), F32), pltpu.VMEM((2 * NORM_ROWS, D), F32)],
        compiler_params=_params(("arbitrary",), 48),
    )(u, u, u, c["dw_w"], c["dw_b"], c["ln_w"], c["ln_b"], c["pw2"], x, mod3, nw)


def _strip_rows(v, strip):
    v = v.reshape(v.shape[:-1] + (v.shape[-1] // strip, 1, strip))
    return jnp.broadcast_to(v, v.shape[:-2] + (SUBLANES, strip))


def _run_trunk(trunk, x, mod_t, h0_all, grid_rows, w):
    state_buf = None
    for i in range(DEPTH):
        kind, j = i % 3, i // 3
        x = _ffn(trunk, x, mod_t[i, 0:3], w["norm_w"][i, 0], w["wg"], w["wu"], w["wd"], i, 0)
        mod3, nw = mod_t[i, 3:6], w["norm_w"][i, 1]
        if kind == 0:
            x = _pool_mixer(trunk, x, mod3, nw, w["pool_w"], w["pool_scale"], j, grid_rows)
        elif kind == 1:
            h0 = None if h0_all is None else h0_all[:, j].reshape(trunk.nseq, 2, SSD_D_INNER, SSD_STATE)
            x, state_buf = _ssd_mixer(trunk, x, mod3, nw, h0, w["ssd"], j, state_buf)
        else:
            u = _cm_in(trunk, x, mod3, nw, w["cm"]["pw1"], j)
            x = _cm_out(trunk, u, w["cm"], j, x, mod3, nw)
        x = _ffn(trunk, x, mod_t[i, 6:9], w["norm_w"][i, 2], w["wg"], w["wu"], w["wd"], i, 1)
    return x, state_buf


def kernel(x_prompt, x_sample, state_ssd, c, c_ctx, norm_w, ada_w, ada_b, ffn_wg, ffn_wu, ffn_wd, pool_w, pool_scale, ssd_in_w, ssd_conv_w, ssd_conv_b, ssd_a_log, ssd_dt_bias, ssd_d, ssd_norm_w, ssd_out_w, cm_pw1, cm_dw_w, cm_dw_b, cm_ln_w, cm_ln_b, cm_pw2):
    n_ctx, l_ctx, _ = x_prompt.shape
    n_lat, l_lat, _ = x_sample.shape
    ctx = _Trunk(n_ctx, l_ctx, shared_cond=True)
    lat = _Trunk(n_lat, l_lat, shared_cond=False)

    cond8 = jnp.zeros((8, D), F32).at[0].set(c_ctx).at[1:1 + n_lat].set(c)
    mod = _ada_mod(cond8, ada_w, ada_b).reshape(DEPTH, 8, N_MOD, D)
    mod = jnp.transpose(mod, (0, 2, 1, 3))[:, :, :, None, :]
    mod_ctx, mod_lat = mod[:, :, 0:1], mod[:, :, 1:1 + n_lat]

    n_ssd = ssd_in_w.shape[0]
    n_cm = cm_pw1.shape[0]
    ndt = 2 * SSD_HEADS
    w = {
        "norm_w": norm_w,
        "wg": ffn_wg.astype(BF16), "wu": ffn_wu.astype(BF16), "wd": ffn_wd.astype(BF16),
        "pool_w": pool_w.astype(BF16),
        "pool_scale": pool_scale[:, None, :],
        "ssd": dict(
            in_w=ssd_in_w.astype(BF16),
            conv_w=_strip_rows(ssd_conv_w, SSD_STRIP).transpose(0, 2, 1, 3, 4),
            conv_b=_strip_rows(ssd_conv_b, SSD_STRIP),
            dt_bias=ssd_dt_bias.reshape(n_ssd, 1, ndt), a_log=ssd_a_log.reshape(n_ssd, 1, ndt),
            d_skip=jnp.repeat(ssd_d, SSD_HEAD_DIM, axis=-1).reshape(n_ssd, SSD_GROUPS, 1, SSD_GROUP_DIM),
            gnorm=ssd_norm_w[:, None, :], out_w=ssd_out_w.astype(BF16)),
        "cm": dict(
            pw1=cm_pw1.astype(BF16),
            dw_w=_strip_rows(cm_dw_w, CM_STRIP).transpose(0, 2, 1, 3, 4),
            dw_b=_strip_rows(cm_dw_b, CM_STRIP),
            ln_w=cm_ln_w[:, None, :], ln_b=cm_ln_b[:, None, :], pw2=cm_pw2.astype(BF16)),
    }

    y_ctx, new_state = _run_trunk(ctx, x_prompt.reshape(ctx.tokens, D), mod_ctx, None, None, w)
    y_lat, _ = _run_trunk(lat, x_sample.reshape(lat.tokens, D), mod_lat, state_ssd,
                          l_lat // GRID_W, w)

    new_state = new_state.reshape(n_ctx, n_ssd, 2, SSD_HEADS, SSD_HEAD_DIM, SSD_STATE)
    return (y_ctx.reshape(x_prompt.shape), y_lat.reshape(x_sample.shape),
            new_state.astype(x_prompt.dtype))
```

```python
import functools

import numpy as np
import jax
import jax.numpy as jnp
from jax import lax
from jax.experimental import pallas as pl
from jax.experimental.pallas import tpu as pltpu

F32 = jnp.float32
BF16 = jnp.bfloat16

D = 2048
DEPTH = 4
N_MOD = 9
D_FF = 5632
EPS = 1e-6
GRID_W = 64
POOL_WINDOWS = (2, 4, 8, 16)
POOL_GROUP_DIM = D // 4
SSD_D_INNER = 2 * D
SSD_HEAD_DIM = 64
SSD_HEADS = SSD_D_INNER // SSD_HEAD_DIM
SSD_GROUPS = 8
SSD_HEADS_PER_GROUP = SSD_HEADS // SSD_GROUPS
SSD_GROUP_DIM = SSD_HEADS_PER_GROUP * SSD_HEAD_DIM
SSD_STATE = 128
SSD_CONV = 5
SSD_CHUNK = 128
SSD_BC_DIM = SSD_GROUPS * SSD_STATE
SSD_ZX_DIM = 2 * SSD_D_INNER + 2 * SSD_BC_DIM
SSD_HALO = 8
SSD_STRIP = SSD_GROUP_DIM
SSD_X_STRIPS = SSD_D_INNER // SSD_STRIP
SSD_BC_STRIPS = 2 * SSD_BC_DIM // SSD_STRIP
CM_KERNEL = 31
CM_HALO = 16
CM_STRIP = 512
POOL_CHUNK = 256
SUBLANES = 8
LANES = 128
NORM_ROWS = 16

MIB = 1024 * 1024


def _silu(x):
    return x * (1.0 / (1.0 + jnp.exp(-x)))


def _sigmoid(x):
    return 1.0 / (1.0 + jnp.exp(-x))


def _rms(x, w):
    return x * lax.rsqrt(jnp.mean(x * x, axis=-1, keepdims=True) + EPS) * w


def _row_loop(n_rows, rows, body):
    def step(i, carry):
        body(pl.ds(pl.multiple_of(i * rows, rows), rows))
        return carry

    lax.fori_loop(0, n_rows // rows, step, 0, unroll=2)


def _inv_rms_to(src_ref, inv_ref):
    x = src_ref[...]
    inv = lax.rsqrt(jnp.mean(x * x, axis=-1, keepdims=True) + EPS)
    inv_ref[...] = jnp.broadcast_to(inv, inv_ref.shape)


def _inv_rows(inv_ref, r, width):
    return jnp.concatenate([inv_ref[r, :]] * (width // LANES), axis=1)


def _pre_rows(x_ref, nw_ref, mod_ref, h_ref, inv_ref, rows_ref):
    _inv_rms_to(x_ref, inv_ref)
    width = x_ref.shape[1]
    rows_ref[0:NORM_ROWS, :] = jnp.broadcast_to(nw_ref[0:1, :] * (1.0 + mod_ref[1, 0]),
                                                (NORM_ROWS, width))
    rows_ref[NORM_ROWS:2 * NORM_ROWS, :] = jnp.broadcast_to(mod_ref[0, 0], (NORM_ROWS, width))

    def body(r):
        y = x_ref[r, :] * _inv_rows(inv_ref, r, width) * rows_ref[0:NORM_ROWS, :]
        h_ref[r, :] = (y + rows_ref[NORM_ROWS:2 * NORM_ROWS, :]).astype(h_ref.dtype)

    _row_loop(x_ref.shape[0], NORM_ROWS, body)


def _post_rows(o_ref, x_ref, src_ref, nw_ref, mod_ref, res_w, inv_ref, rows_ref):
    _inv_rms_to(src_ref, inv_ref)
    width = x_ref.shape[1]
    gate = mod_ref[2, 0] if res_w == 1.0 else res_w * mod_ref[2, 0]
    rows_ref[0:NORM_ROWS, :] = jnp.broadcast_to(gate * nw_ref[1:2, :], (NORM_ROWS, width))

    def body(r):
        y = src_ref[r, :] * _inv_rows(inv_ref, r, width) * rows_ref[0:NORM_ROWS, :]
        o_ref[r, :] = x_ref[r, :] + y

    _row_loop(x_ref.shape[0], NORM_ROWS, body)


def _dot(a, b):
    return jnp.dot(a, b, preferred_element_type=F32)


def _dot_nt(a, b):
    return lax.dot_general(a, b, (((1,), (1,)), ((), ())), preferred_element_type=F32)


def _dot_tn(a, b):
    return lax.dot_general(a, b, (((0,), (0,)), ((), ())), preferred_element_type=F32)


def _split_bf16(v):
    hi = v.astype(BF16)
    lo = (v - hi.astype(F32)).astype(BF16)
    return hi, lo


def _params(semantics, vmem_mib):
    return pltpu.CompilerParams(dimension_semantics=semantics,
                                vmem_limit_bytes=vmem_mib * MIB)


class _Trunk:
    def __init__(self, nseq, seq_len, shared_cond):
        self.nseq = nseq
        self.seq_len = seq_len
        self.tokens = nseq * seq_len
        self.shared_cond = shared_cond

    def cond_row(self, tm):
        if self.shared_cond:
            return lambda i: 0
        per = self.seq_len // tm
        return lambda i: i // per

    def mod_spec(self, tm):
        row = self.cond_row(tm)
        return pl.BlockSpec((3, 1, 1, D), lambda i, *_: (0, row(i), 0, 0))


def _ada_kernel(c_ref, w_ref, b_ref, o_ref):
    sc = _silu(c_ref[...]).astype(BF16)
    o_ref[0] = _dot(sc, w_ref[0].astype(BF16)) + b_ref[0]


def _ada_mod(cond8, ada_w, ada_b):
    tn = 1024
    n = N_MOD * D
    return pl.pallas_call(
        _ada_kernel,
        grid=(DEPTH, n // tn),
        in_specs=[
            pl.BlockSpec((8, D), lambda l, j: (0, 0)),
            pl.BlockSpec((1, D, tn), lambda l, j: (l, 0, j)),
            pl.BlockSpec((1, 1, tn), lambda l, j: (l, 0, j)),
        ],
        out_specs=pl.BlockSpec((1, 8, tn), lambda l, j: (l, 0, j)),
        out_shape=jax.ShapeDtypeStruct((DEPTH, 8, n), F32),
        compiler_params=_params(("arbitrary", "arbitrary"), 52),
    )(cond8, ada_w, ada_b.reshape(DEPTH, 1, n))


def _ffn_kernel(x_ref, mod_ref, nw_ref, wg_ref, wu_ref, wd_ref, *rest, cast_next):
    if cast_next:
        srcs, rest = rest[:3], rest[3:]
        o_ref, dsts, rest = rest[0], rest[1:4], rest[4:]
        for src, dst in zip(srcs, dsts):
            dst[...] = src[...].astype(BF16)
    else:
        o_ref, rest = rest[0], rest[1:]
    h_ref, inv_ref, rows_ref = rest
    j = pl.program_id(1)

    @pl.when(j == 0)
    def _():
        _pre_rows(x_ref, nw_ref, mod_ref, h_ref, inv_ref, rows_ref)
        o_ref[...] = jnp.zeros_like(o_ref)

    h = h_ref[...]
    g = _dot(h, wg_ref[...])
    u = _dot(h, wu_ref[...])
    a = (_silu(g) * u).astype(BF16)
    o_ref[...] += _dot(a, wd_ref[...])

    @pl.when(j == pl.num_programs(1) - 1)
    def _():
        _post_rows(o_ref, x_ref, o_ref, nw_ref, mod_ref, 0.5, inv_ref, rows_ref)


def _ffn(trunk, x, mod3, nw, weights, cast_next=None):
    tm, tf, vmem_mib = (1024, 512, 58) if cast_next is None else (512, 512, 48)
    nblk = trunk.tokens // tm
    in_specs = [
        pl.BlockSpec((tm, D), lambda i, j: (i, 0)),
        trunk.mod_spec(tm),
        pl.BlockSpec((2, D), lambda i, j: (0, 0)),
        pl.BlockSpec((D, tf), lambda i, j: (0, j)),
        pl.BlockSpec((D, tf), lambda i, j: (0, j)),
        pl.BlockSpec((tf, D), lambda i, j: (j, 0)),
    ]
    inputs = [x, mod3, nw, *weights]
    out_specs = [pl.BlockSpec((tm, D), lambda i, j: (i, 0))]
    out_shape = [jax.ShapeDtypeStruct((trunk.tokens, D), F32)]
    if cast_next is not None:
        fg, fu, fd, layer, half = cast_next
        td = D // nblk
        in_specs += [
            pl.BlockSpec((None, None, td, tf), lambda i, j: (layer, half, i, j)),
            pl.BlockSpec((None, None, td, tf), lambda i, j: (layer, half, i, j)),
            pl.BlockSpec((None, None, tf, td), lambda i, j: (layer, half, j, i)),
        ]
        inputs += [fg, fu, fd]
        out_specs += [pl.BlockSpec((td, tf), lambda i, j: (i, j)),
                      pl.BlockSpec((td, tf), lambda i, j: (i, j)),
                      pl.BlockSpec((tf, td), lambda i, j: (j, i))]
        out_shape += [jax.ShapeDtypeStruct((D, D_FF), BF16), jax.ShapeDtypeStruct((D, D_FF), BF16),
                      jax.ShapeDtypeStruct((D_FF, D), BF16)]
    res = pl.pallas_call(
        functools.partial(_ffn_kernel, cast_next=cast_next is not None),
        grid=(nblk, D_FF // tf),
        in_specs=in_specs, out_specs=out_specs, out_shape=out_shape,
        scratch_shapes=[pltpu.VMEM((tm, D), BF16), pltpu.VMEM((tm, LANES), F32),
                        pltpu.VMEM((2 * NORM_ROWS, D), F32)],
        compiler_params=_params(("arbitrary", "arbitrary"), vmem_mib),
    )(*inputs)
    return res[0], (tuple(res[1:]) if cast_next is not None else None)


def _window_bounds(n, w):
    pos = np.arange(n)
    lo = np.clip(pos - w // 2, 0, n)
    hi = np.clip(pos + (w - w // 2), 0, n)
    return lo, hi


def _band_matrix(n, w):
    lo, hi = _window_bounds(n, w)
    col = np.arange(n)[None, :]
    return ((col >= lo[:, None]) & (col < hi[:, None])).astype(np.float32)


def _window_sum(a, v):
    hi, lo = _split_bf16(v)
    return _dot(a, hi) + _dot(a, lo)


def _inv_rms(x_ref):
    x = x_ref[...]
    return lax.rsqrt(jnp.mean(x * x, axis=-1, keepdims=True) + EPS)


def _pre_slice(x_ref, inv, nw_ref, mod_ref, rows, cols):
    y = x_ref[rows, cols] * inv[rows, :] * nw_ref[0:1, cols]
    return y * (1.0 + mod_ref[1, 0, :, cols]) + mod_ref[0, 0, :, cols]


def _pool_ctx_kernel(x_ref, mod_ref, nw_ref, a_ref, ic_ref, pw_ref, ps_ref, o_ref, mix_ref, inv_ref,
                     rows_ref, *, tm):
    inv = _inv_rms(x_ref)
    gd = POOL_GROUP_DIM
    for g in range(len(POOL_WINDOWS)):
        cols = slice(g * gd, (g + 1) * gd)
        a = a_ref[g]
        for q in range(tm // POOL_CHUNK):
            rows = slice(q * POOL_CHUNK, (q + 1) * POOL_CHUNK)
            v = _pre_slice(x_ref, inv, nw_ref, mod_ref, rows, cols)
            d = (_window_sum(a, v) * ic_ref[g] - v).astype(BF16)
            mix_ref[rows, cols] = _dot(d, pw_ref[g]) * ps_ref[:, cols]
    _post_rows(o_ref, x_ref, mix_ref, nw_ref, mod_ref, 1.0, inv_ref, rows_ref)


def _pool_lat_kernel(xp_ref, xc_ref, xn_ref, mod_ref, nw_ref, a_ref, ic_ref, pw_ref, ps_ref,
                     o_ref, mix_ref, inv_ref, rows_ref, *, blocks_per_seq):
    jb = pl.program_id(0) % blocks_per_seq
    valid_p = jb > 0
    valid_n = jb < blocks_per_seq - 1
    x_refs = (xp_ref, xc_ref, xn_ref)
    invs = tuple(_inv_rms(r) for r in x_refs)
    gd = POOL_GROUP_DIM
    rows_per_chunk = POOL_CHUNK // GRID_W
    rows_per_block = xc_ref.shape[0] // GRID_W
    chunks_per_block = rows_per_block // rows_per_chunk
    for g, w in enumerate(POOL_WINDOWS):
        before, after = w // 2, w - w // 2 - 1
        cols = slice(g * gd, (g + 1) * gd)
        a = a_ref[g]
        first_row = rows_per_block - before
        last_row = 2 * rows_per_block - 1 + after
        cm = {}
        for ch in range(first_row // rows_per_chunk, last_row // rows_per_chunk + 1):
            which = ch // chunks_per_block
            off = (ch % chunks_per_block) * POOL_CHUNK
            v = _pre_slice(x_refs[which], invs[which], nw_ref, mod_ref,
                           slice(off, off + POOL_CHUNK), cols)
            s = _window_sum(a, v)
            if ch < chunks_per_block:
                s = jnp.where(valid_p, s, 0.0)
            elif ch >= 2 * chunks_per_block:
                s = jnp.where(valid_n, s, 0.0)
            cm[ch] = s

        def slab(r):
            o = (r % rows_per_chunk) * GRID_W
            return cm[r // rows_per_chunk][o:o + GRID_W, :]

        outs = []
        for r in range(rows_per_block, 2 * rows_per_block):
            acc = slab(r - before)
            for k in range(-before + 1, after + 1):
                acc = acc + slab(r + k)
            outs.append(acc)
        m = jnp.concatenate(outs, axis=0) * ic_ref[0, g]
        v = _pre_slice(xc_ref, invs[1], nw_ref, mod_ref, slice(None), cols)
        d = (m - v).astype(BF16)
        mix_ref[:, cols] = _dot(d, pw_ref[g]) * ps_ref[:, cols]
    _post_rows(o_ref, xc_ref, mix_ref, nw_ref, mod_ref, 1.0, inv_ref, rows_ref)


def _pool_ctx_consts(seq_len):
    mats, inv = [], []
    for w in POOL_WINDOWS:
        lo, hi = _window_bounds(seq_len, w)
        mats.append(_band_matrix(seq_len, w))
        inv.append((1.0 / (hi - lo)).astype(np.float32)[:, None])
    return jnp.asarray(np.stack(mats), BF16), jnp.asarray(np.stack(inv), F32)


def _pool_lat_consts(rows, tm):
    rows_per_block = tm // GRID_W
    mats, inv = [], []
    for w in POOL_WINDOWS:
        mats.append(np.kron(np.eye(POOL_CHUNK // GRID_W, dtype=np.float32), _band_matrix(GRID_W, w)))
        lo_c, hi_c = _window_bounds(GRID_W, w)
        lo_r, hi_r = _window_bounds(rows, w)
        cnt = (hi_r - lo_r)[:, None] * (hi_c - lo_c)[None, :]
        inv.append((1.0 / cnt).astype(np.float32).reshape(rows // rows_per_block, tm, 1))
    inv = np.stack(inv, axis=1)
    return jnp.asarray(np.stack(mats), BF16), jnp.asarray(inv, F32)


def _pool_mixer(trunk, x, mod3, nw, pool_w, pool_scale, layer, grid_rows):
    tm = 512
    nblk = trunk.tokens // tm
    gd = POOL_GROUP_DIM
    common_specs = [
        trunk.mod_spec(tm),
        pl.BlockSpec((2, D), lambda i: (0, 0)),
        pl.BlockSpec((4, POOL_CHUNK, POOL_CHUNK), lambda i: (0, 0, 0)),
    ]
    tail_specs = [
        pl.BlockSpec((None, 4, gd, gd), lambda i: (layer, 0, 0, 0)),
        pl.BlockSpec((None, 1, D), lambda i: (layer, 0, 0)),
    ]
    out_spec = pl.BlockSpec((tm, D), lambda i: (i, 0))
    out_shape = jax.ShapeDtypeStruct((trunk.tokens, D), F32)
    scratch = [pltpu.VMEM((tm, D), F32), pltpu.VMEM((tm, LANES), F32), pltpu.VMEM((2 * NORM_ROWS, D), F32)]
    if grid_rows is None:
        assert trunk.seq_len == POOL_CHUNK
        a, ic = _pool_ctx_consts(trunk.seq_len)
        return pl.pallas_call(
            functools.partial(_pool_ctx_kernel, tm=tm),
            grid=(nblk,),
            in_specs=[pl.BlockSpec((tm, D), lambda i: (i, 0))] + common_specs
            + [pl.BlockSpec((4, POOL_CHUNK, 1), lambda i: (0, 0, 0))] + tail_specs,
            out_specs=out_spec, out_shape=out_shape, scratch_shapes=scratch,
            compiler_params=_params(("arbitrary",), 48),
        )(x, mod3, nw, a, ic, pool_w, pool_scale)
    blocks_per_seq = trunk.seq_len // tm
    assert max(POOL_WINDOWS) // 2 <= tm // GRID_W
    a, ic = _pool_lat_consts(grid_rows, tm)
    return pl.pallas_call(
        functools.partial(_pool_lat_kernel, blocks_per_seq=blocks_per_seq),
        grid=(nblk,),
        in_specs=[
            pl.BlockSpec((tm, D), lambda i: (jnp.maximum(i - 1, 0), 0)),
            pl.BlockSpec((tm, D), lambda i: (i, 0)),
            pl.BlockSpec((tm, D), lambda i: (jnp.minimum(i + 1, nblk - 1), 0)),
        ] + common_specs
        + [pl.BlockSpec((1, 4, tm, 1), lambda i: (i % blocks_per_seq, 0, 0, 0))] + tail_specs,
        out_specs=out_spec, out_shape=out_shape, scratch_shapes=scratch,
        compiler_params=_params(("arbitrary",), 56),
    )(x, x, x, mod3, nw, a, ic, pool_w, pool_scale)


def _ssd_in_kernel(x_ref, mod_ref, nw_ref, w_ref, wdt_ref, zx_ref, dt_ref, h_ref, inv_ref, rows_ref):
    @pl.when(pl.program_id(1) == 0)
    def _():
        _pre_rows(x_ref, nw_ref, mod_ref, h_ref, inv_ref, rows_ref)
        dt_ref[...] = _dot(h_ref[...], wdt_ref[...])

    zx_ref[...] = _dot(h_ref[...], w_ref[...])


def _ssd_in(trunk, x, mod3, nw, in_w, layer):
    tm, tn = 512, 2048
    ndt = 2 * SSD_HEADS
    return pl.pallas_call(
        _ssd_in_kernel,
        grid=(trunk.tokens // tm, SSD_ZX_DIM // tn),
        in_specs=[
            pl.BlockSpec((tm, D), lambda i, j: (i, 0)),
            trunk.mod_spec(tm),
            pl.BlockSpec((2, D), lambda i, j: (0, 0)),
            pl.BlockSpec((None, D, tn), lambda i, j: (layer, 0, j)),
            pl.BlockSpec((None, D, ndt), lambda i, j: (layer, 0, SSD_ZX_DIM // ndt)),
        ],
        out_specs=[
            pl.BlockSpec((tm, tn), lambda i, j: (i, j)),
            pl.BlockSpec((tm, ndt), lambda i, j: (i, 0)),
        ],
        out_shape=[
            jax.ShapeDtypeStruct((trunk.tokens, SSD_ZX_DIM), F32),
            jax.ShapeDtypeStruct((trunk.tokens, ndt), F32),
        ],
        scratch_shapes=[pltpu.VMEM((tm, D), BF16), pltpu.VMEM((tm, LANES), F32), pltpu.VMEM((2 * NORM_ROWS, D), F32)],
        compiler_params=_params(("arbitrary", "arbitrary"), 52),
    )(x, mod3, nw, in_w, in_w)


def _ssd_conv_strips(buf_ref, w_ref, b_ref, first, count, store):
    base = SSD_HALO - SSD_CONV // 2
    rb = 4 * SUBLANES

    def strip(s, carry):
        for r in range(0, SSD_CHUNK, rb):
            acc = b_ref[s][None]
            for t in range(SSD_CONV):
                x = buf_ref[s, base + t + r:base + t + r + rb, :]
                acc = acc + w_ref[s, t][None] * x.reshape(rb // SUBLANES, SUBLANES, SSD_STRIP)
            store(s, r, _silu(acc).reshape(rb, SSD_STRIP))
        return carry

    lax.fori_loop(first, first + count, strip, 0)


def _ssd_scan_kernel(*refs, reverse, nc, has_h0, emit_state, alias_state):
    it = iter(refs)
    if not reverse:
        raw = [(next(it), next(it), next(it)) for _ in range(3)]
        cw_ref, cbias_ref = next(it), next(it)
    else:
        xsc_ref, bcc_ref = next(it), next(it)
    dt_ref, dtb_ref, alog_ref, e_ref = next(it), next(it), next(it), next(it)
    h0_ref = next(it) if has_h0 else None
    if reverse:
        yf_ref, z_ref, gn_ref = next(it), next(it), next(it)
    else:
        dskip_ref = next(it)
    if alias_state:
        next(it)
    if reverse:
        yn_ref = next(it)
    else:
        yf_ref, xsc_ref, bcc_ref = next(it), next(it), next(it)
    so_ref = next(it) if emit_state else None
    state_ref, acol_s, arow_s, dtrow_s, lhs_off_s, lhs_w_s = (next(it) for _ in range(6))
    if reverse:
        y_s, ytmp_ref, inv_ref = next(it), next(it), next(it)
    else:
        xbuf_ref, bct_ref = next(it), next(it)

    t = SSD_CHUNK
    step = pl.program_id(1)
    cc = (nc - 1 - step) if reverse else step
    gdim = SSD_GROUP_DIM

    @pl.when(step == 0)
    def _():
        if has_h0:
            for g in range(SSD_GROUPS):
                for k in range(gdim // t):
                    r0 = g * gdim + k * t
                    state_ref[g, :, k * t:(k + 1) * t] = h0_ref[0, 0, r0:r0 + t, :].T
        else:
            state_ref[...] = jnp.zeros_like(state_ref)

    if not reverse:
        has_prev = cc > 0
        has_next = cc < nc - 1
        s0 = 0
        for cur, prev, nxt in raw:
            for k in range(cur.shape[1] // SSD_STRIP):
                lanes = slice(k * SSD_STRIP, (k + 1) * SSD_STRIP)
                xbuf_ref[s0 + k, 0:SSD_HALO, :] = jnp.where(has_prev, prev[:, lanes], 0.0)
                xbuf_ref[s0 + k, SSD_HALO:SSD_HALO + t, :] = cur[:, lanes]
                xbuf_ref[s0 + k, SSD_HALO + t:SSD_HALO + t + SSD_HALO, :] = jnp.where(
                    has_next, nxt[:, lanes], 0.0)
            s0 += cur.shape[1] // SSD_STRIP

        def store_x(s, r, v):
            xsc_ref[0, s, r:r + v.shape[0], :] = v

        def store_bc(s, r, v):
            bct_ref[s - SSD_X_STRIPS, r:r + v.shape[0], :] = v

        _ssd_conv_strips(xbuf_ref, cw_ref, cbias_ref, 0, SSD_X_STRIPS, store_x)
        _ssd_conv_strips(xbuf_ref, cw_ref, cbias_ref, SSD_X_STRIPS, SSD_BC_STRIPS, store_bc)
        per = SSD_STRIP // SSD_STATE
        for k in range(2 * SSD_GROUPS):
            bcc_ref[0, k] = bct_ref[k // per, :, (k % per) * SSD_STATE:(k % per + 1) * SSD_STATE].astype(BF16)

    dt_raw = dt_ref[...] + dtb_ref[...]
    dt = jnp.maximum(dt_raw, 0.0) + jnp.log1p(jnp.exp(-jnp.abs(dt_raw)))
    dta = dt * (-jnp.exp(alog_ref[...]))

    row = lax.broadcasted_iota(jnp.int32, (t, t), 0)
    col = lax.broadcasted_iota(jnp.int32, (t, t), 1)
    mask = (col >= row) if reverse else (col <= row)
    hi = dta.astype(BF16)
    mid, lo = _split_bf16(dta - hi.astype(F32))
    tri = jnp.where(mask, 1.0, 0.0).astype(BF16)
    parts = _dot(tri, jnp.concatenate([hi, mid, lo], axis=1))
    acum = parts[:, 0:t] + parts[:, t:2 * t] + parts[:, 2 * t:3 * t]
    acum_t = acum.T
    dt_t = dt.T
    end = 0 if reverse else t - 1
    d0 = SSD_HEADS if reverse else 0
    mine = (col >= d0) & (col < d0 + SSD_HEADS)

    hi, lo = _split_bf16(jnp.where(mine, jnp.exp(acum), 0.0))
    lhs_off_s[...] = jnp.concatenate([hi, lo], axis=1)
    hi, lo = _split_bf16(jnp.where(mine, dt * jnp.exp(acum[end:end + 1, :] - acum), 0.0))
    lhs_w_s[...] = jnp.concatenate([hi, lo], axis=1)

    hpg = SSD_HEADS_PER_GROUP
    for g in range(SSD_GROUPS):
        heads = slice(d0 + g * hpg, d0 + (g + 1) * hpg)
        acol_s[g] = acum[:, heads]
        arow_s[g] = acum_t[heads, :]
        dtrow_s[g] = dt_t[heads, :]

    pair = 2 * SSD_HEAD_DIM
    first_head = lax.broadcasted_iota(jnp.int32, (t, pair), 1) < SSD_HEAD_DIM

    def group_body(g, carry):
        xg = xsc_ref[0, g]
        bg = bcc_ref[0, g]
        cg = bcc_ref[0, SSD_GROUPS + g]
        acol = acol_s[g]
        arow = arow_s[g]
        dtrow = dtrow_s[g]
        off = _dot(lhs_off_s[...], e_ref[g])
        wgt = _dot(lhs_w_s[...], e_ref[g])
        cb = _dot_nt(cg, bg)
        st = state_ref[g]
        y_off = _dot(cg, st.astype(BF16))
        new = _dot_tn(bg, (xg * wgt).astype(BF16))
        state_ref[g] = st * off[end:end + 1, :] + new
        ys = []
        for p in range(hpg // 2):
            lanes = slice(p * pair, (p + 1) * pair)
            sc = []
            for r in (2 * p, 2 * p + 1):
                lm = jnp.exp(jnp.where(mask, acol[:, r:r + 1] - arow[r:r + 1, :], -jnp.inf))
                sc.append((cb * lm * dtrow[r:r + 1, :]).astype(BF16))
            xp = xg[:, lanes]
            x_bd = jnp.concatenate([jnp.where(first_head, xp, 0.0),
                                    jnp.where(first_head, 0.0, xp)], axis=0).astype(BF16)
            ys.append(_dot(jnp.concatenate(sc, axis=1), x_bd) + y_off[:, lanes] * off[:, lanes])
        y = jnp.concatenate(ys, axis=1)
        if reverse:
            y_s[g] = y
        else:
            yf_ref[0, g] = y + dskip_ref[g] * xg
        return carry

    lax.fori_loop(0, SSD_GROUPS, group_body, 0, unroll=4)

    if reverse:
        for g in range(SSD_GROUPS):
            lanes = slice(g * gdim, (g + 1) * gdim)
            ytmp_ref[:, lanes] = (yf_ref[0, g] + y_s[g]) * _silu(z_ref[:, lanes])
        _inv_rms_to(ytmp_ref, inv_ref)

        def norm(r):
            y = ytmp_ref[r, :] * _inv_rows(inv_ref, r, SSD_D_INNER) * gn_ref[...]
            yn_ref[r, :] = y.astype(yn_ref.dtype)

        _row_loop(t, NORM_ROWS, norm)

    if emit_state:
        @pl.when(step == nc - 1)
        def _():
            for g in range(SSD_GROUPS):
                for k in range(gdim // t):
                    r0 = g * gdim + k * t
                    so_ref[0, 0, 0, r0:r0 + t, :] = state_ref[g, :, k * t:(k + 1) * t].T


def _ssd_expand_consts():
    e = np.zeros((2, SSD_GROUPS, 4 * SSD_HEADS, SSD_GROUP_DIM), np.float32)
    for d in range(2):
        for h in range(SSD_HEADS):
            g, r = divmod(h, SSD_HEADS_PER_GROUP)
            for half in range(2):
                e[d, g, half * 2 * SSD_HEADS + d * SSD_HEADS + h,
                  r * SSD_HEAD_DIM:(r + 1) * SSD_HEAD_DIM] = 1.0
    return jnp.asarray(e, BF16)


def _ssd_scan(trunk, p, layer, dt, *, reverse, zx=None, conv=None, y_fwd=None, h0=None,
              state_buf=None):
    t = SSD_CHUNK
    nc = trunk.seq_len // t
    nchunks = trunk.tokens // t
    emit_state = h0 is None
    alias_state = emit_state and state_buf is not None
    hb = t // SSD_HALO
    last8 = trunk.tokens // SSD_HALO - 1
    di, bc = SSD_D_INNER, SSD_BC_DIM
    n_strips = SSD_X_STRIPS + SSD_BC_STRIPS
    ndt = 2 * SSD_HEADS
    direction = 1 if reverse else 0

    def blk(b, c):
        return b * nc + ((nc - 1 - c) if reverse else c)

    def full(shape):
        return pl.BlockSpec(shape, lambda b, c: (0,) * len(shape))

    def chunk_major(lead, rows, lanes):
        return pl.BlockSpec((1, lead, rows, lanes), lambda b, c: (blk(b, c), 0, 0, 0))

    inputs, specs = [], []

    def add(arr, spec):
        inputs.append(arr)
        specs.append(spec)

    if not reverse:
        for width, colb in ((di, 1), (bc, 2 * di // bc), (bc, 2 * di // bc + 1)):
            add(zx, pl.BlockSpec((t, width), lambda b, c, colb=colb: (blk(b, c), colb)))
            add(zx, pl.BlockSpec((SSD_HALO, width),
                                 lambda b, c, colb=colb: (jnp.maximum(blk(b, c) * hb - 1, 0), colb)))
            add(zx, pl.BlockSpec((SSD_HALO, width),
                                 lambda b, c, colb=colb: (jnp.minimum(blk(b, c) * hb + hb, last8), colb)))
        add(p["conv_w"], pl.BlockSpec((None, n_strips, SSD_CONV, SUBLANES, SSD_STRIP),
                                      lambda b, c: (layer, 0, 0, 0, 0)))
        add(p["conv_b"], pl.BlockSpec((None, n_strips, SUBLANES, SSD_STRIP),
                                      lambda b, c: (layer, 0, 0, 0)))
    else:
        add(conv[0], chunk_major(SSD_GROUPS, t, SSD_GROUP_DIM))
        add(conv[1], chunk_major(2 * SSD_GROUPS, t, SSD_STATE))
    add(dt, pl.BlockSpec((t, ndt), lambda b, c: (blk(b, c), 0)))
    add(p["dt_bias"], pl.BlockSpec((None, 1, ndt), lambda b, c: (layer, 0, 0)))
    add(p["a_log"], pl.BlockSpec((None, 1, ndt), lambda b, c: (layer, 0, 0)))
    add(_ssd_expand_consts(), pl.BlockSpec((None, SSD_GROUPS, 2 * ndt, SSD_GROUP_DIM),
                                           lambda b, c: (direction, 0, 0, 0)))
    if h0 is not None:
        add(h0, pl.BlockSpec((1, 1, di, SSD_STATE), lambda b, c: (b, direction, 0, 0)))
    if reverse:
        add(y_fwd, chunk_major(SSD_GROUPS, t, SSD_GROUP_DIM))
        add(zx, pl.BlockSpec((t, di), lambda b, c: (blk(b, c), 0)))
        add(p["gnorm"], pl.BlockSpec((None, 1, di), lambda b, c: (layer, 0, 0)))
    else:
        add(p["d_skip"], pl.BlockSpec((None, SSD_GROUPS, 1, SSD_GROUP_DIM), lambda b, c: (layer, 0, 0, 0)))
    aliases = {}
    if alias_state:
        aliases[len(inputs)] = 1 if reverse else 3
        add(state_buf, pl.BlockSpec(memory_space=pl.ANY))

    if reverse:
        out_specs = [pl.BlockSpec((t, di), lambda b, c: (blk(b, c), 0))]
        out_shape = [jax.ShapeDtypeStruct((trunk.tokens, di), BF16)]
    else:
        out_specs = [chunk_major(SSD_GROUPS, t, SSD_GROUP_DIM),
                     chunk_major(SSD_GROUPS, t, SSD_GROUP_DIM),
                     chunk_major(2 * SSD_GROUPS, t, SSD_STATE)]
        out_shape = [jax.ShapeDtypeStruct((nchunks, SSD_GROUPS, t, SSD_GROUP_DIM), F32),
                     jax.ShapeDtypeStruct((nchunks, SSD_GROUPS, t, SSD_GROUP_DIM), F32),
                     jax.ShapeDtypeStruct((nchunks, 2 * SSD_GROUPS, t, SSD_STATE), BF16)]
    if emit_state:
        n_layers = p["a_log"].shape[0]
        out_specs.append(pl.BlockSpec((1, 1, 1, di, SSD_STATE),
                                      lambda b, c: (b, layer, direction, 0, 0)))
        out_shape.append(jax.ShapeDtypeStruct((trunk.nseq, n_layers, 2, di, SSD_STATE), F32))

    hpg = SSD_HEADS_PER_GROUP
    scratch = [
        pltpu.VMEM((SSD_GROUPS, SSD_STATE, SSD_GROUP_DIM), F32),
        pltpu.VMEM((SSD_GROUPS, t, hpg), F32),
        pltpu.VMEM((SSD_GROUPS, hpg, t), F32),
        pltpu.VMEM((SSD_GROUPS, hpg, t), F32),
        pltpu.VMEM((t, 2 * ndt), BF16),
        pltpu.VMEM((t, 2 * ndt), BF16),
    ]
    if reverse:
        scratch += [pltpu.VMEM((SSD_GROUPS, t, SSD_GROUP_DIM), F32), pltpu.VMEM((t, di), F32),
                    pltpu.VMEM((t, LANES), F32)]
    else:
        scratch += [pltpu.VMEM((n_strips, t + 2 * SSD_HALO, SSD_STRIP), F32),
                    pltpu.VMEM((SSD_BC_STRIPS, t, SSD_STRIP), F32)]
    res = pl.pallas_call(
        functools.partial(_ssd_scan_kernel, reverse=reverse, nc=nc, has_h0=h0 is not None,
                          emit_state=emit_state, alias_state=alias_state),
        grid=(trunk.nseq, nc),
        in_specs=specs, out_specs=out_specs, out_shape=out_shape, scratch_shapes=scratch,
        input_output_aliases=aliases,
        compiler_params=_params(("arbitrary", "arbitrary"), 48),
    )(*inputs)
    return res


def _mm_post_kernel(a_ref, w_ref, x_ref, mod_ref, nw_ref, o_ref, acc_ref, inv_ref, rows_ref):
    k = pl.program_id(1)

    @pl.when(k == 0)
    def _():
        acc_ref[...] = jnp.zeros_like(acc_ref)

    acc_ref[...] += _dot(a_ref[...], w_ref[...])

    @pl.when(k == pl.num_programs(1) - 1)
    def _():
        _post_rows(o_ref, x_ref, acc_ref, nw_ref, mod_ref, 1.0, inv_ref, rows_ref)


def _mm_post(trunk, a, w, layer, x, mod3, nw):
    tm, tk = 512, 2048
    kdim = a.shape[1]
    return pl.pallas_call(
        _mm_post_kernel,
        grid=(trunk.tokens // tm, kdim // tk),
        in_specs=[
            pl.BlockSpec((tm, tk), lambda i, k: (i, k)),
            pl.BlockSpec((None, tk, D), lambda i, k: (layer, k, 0)),
            pl.BlockSpec((tm, D), lambda i, k: (i, 0)),
            trunk.mod_spec(tm),
            pl.BlockSpec((2, D), lambda i, k: (0, 0)),
        ],
        out_specs=pl.BlockSpec((tm, D), lambda i, k: (i, 0)),
        out_shape=jax.ShapeDtypeStruct((trunk.tokens, D), F32),
        scratch_shapes=[pltpu.VMEM((tm, D), F32), pltpu.VMEM((tm, LANES), F32), pltpu.VMEM((2 * NORM_ROWS, D), F32)],
        compiler_params=_params(("arbitrary", "arbitrary"), 52),
    )(a, w, x, mod3, nw)


def _ssd_mixer(trunk, x, mod3, nw, h0, p, layer, state_buf):
    zx, dt = _ssd_in(trunk, x, mod3, nw, p["in_w"], layer)
    res = _ssd_scan(trunk, p, layer, dt, reverse=False, zx=zx, h0=h0, state_buf=state_buf)
    y_f, conv = res[0], (res[1], res[2])
    if h0 is None:
        state_buf = res[3]
    res = _ssd_scan(trunk, p, layer, dt, reverse=True, zx=zx, conv=conv, y_fwd=y_f, h0=h0,
                    state_buf=state_buf)
    if h0 is None:
        state_buf = res[1]
    return _mm_post(trunk, res[0], p["out_w"], layer, x, mod3, nw), state_buf


def _cm_in_kernel(x_ref, mod_ref, nw_ref, wa_ref, wb_ref, o_ref, h_ref, inv_ref, rows_ref):
    @pl.when(pl.program_id(1) == 0)
    def _():
        _pre_rows(x_ref, nw_ref, mod_ref, h_ref, inv_ref, rows_ref)

    h = h_ref[...]
    o_ref[...] = _dot(h, wa_ref[...]) * _sigmoid(_dot(h, wb_ref[...]))


def _cm_in(trunk, x, mod3, nw, pw1, layer):
    tm, tn = 512, 1024
    nb = D // tn
    return pl.pallas_call(
        _cm_in_kernel,
        grid=(trunk.tokens // tm, nb),
        in_specs=[
            pl.BlockSpec((tm, D), lambda i, j: (i, 0)),
            trunk.mod_spec(tm),
            pl.BlockSpec((2, D), lambda i, j: (0, 0)),
            pl.BlockSpec((None, D, tn), lambda i, j: (layer, 0, j)),
            pl.BlockSpec((None, D, tn), lambda i, j: (layer, 0, j + nb)),
        ],
        out_specs=pl.BlockSpec((tm, tn), lambda i, j: (i, j)),
        out_shape=jax.ShapeDtypeStruct((trunk.tokens, D), F32),
        scratch_shapes=[pltpu.VMEM((tm, D), BF16), pltpu.VMEM((tm, LANES), F32), pltpu.VMEM((2 * NORM_ROWS, D), F32)],
        compiler_params=_params(("arbitrary", "arbitrary"), 52),
    )(x, mod3, nw, pw1, pw1)


def _cm_out_kernel(u_ref, up_ref, un_ref, dw_ref, db_ref, lw_ref, lb_ref, w2_ref, x_ref, mod_ref,
                   nw_ref, o_ref, buf_ref, sh_ref, cv_ref, v_ref, out_ref, inv_ref, rows_ref, *,
                   blocks_per_seq):
    tm = u_ref.shape[0]
    jb = pl.program_id(0) % blocks_per_seq
    n_strips = D // CM_STRIP
    for s in range(n_strips):
        lanes = slice(s * CM_STRIP, (s + 1) * CM_STRIP)
        buf_ref[s, 0:CM_HALO, :] = jnp.where(jb > 0, up_ref[:, lanes], 0.0)
        buf_ref[s, CM_HALO:CM_HALO + tm, :] = u_ref[:, lanes]
        buf_ref[s, CM_HALO + tm:CM_HALO + tm + CM_HALO, :] = jnp.where(
            jb < blocks_per_seq - 1, un_ref[:, lanes], 0.0)

    base = CM_HALO - CM_KERNEL // 2
    span = tm + (base + CM_KERNEL - 1) // SUBLANES * SUBLANES
    rb = 4 * SUBLANES

    def strip(s, carry):
        for r in range(1, SUBLANES):
            sh_ref[r - 1] = buf_ref[s, r:r + span, :]
        for r0 in range(0, tm, rb):
            acc = db_ref[s][None]
            for t in range(CM_KERNEL):
                q, r = divmod(base + t, SUBLANES)
                lo = r0 + q * SUBLANES
                x = buf_ref[s, lo:lo + rb, :] if r == 0 else sh_ref[r - 1, lo:lo + rb, :]
                acc = acc + dw_ref[s, t][None] * x.reshape(rb // SUBLANES, SUBLANES, CM_STRIP)
            cv_ref[s, r0:r0 + rb, :] = acc.reshape(rb, CM_STRIP)
        return carry

    lax.fori_loop(0, n_strips, strip, 0)

    total = cv_ref[0]
    for s in range(1, n_strips):
        total = total + cv_ref[s]
    mu = jnp.sum(total, axis=-1, keepdims=True) * (1.0 / D)
    sq = None
    for s in range(n_strips):
        cen = cv_ref[s] - mu
        sq = cen * cen if sq is None else sq + cen * cen
    inv = lax.rsqrt(jnp.sum(sq, axis=-1, keepdims=True) * (1.0 / D) + EPS)
    for s in range(n_strips):
        lanes = slice(s * CM_STRIP, (s + 1) * CM_STRIP)
        y = (cv_ref[s] - mu) * inv * lw_ref[:, lanes] + lb_ref[:, lanes]
        v_ref[:, lanes] = _silu(y).astype(BF16)
    out_ref[...] = _dot(v_ref[...], w2_ref[...])
    _post_rows(o_ref, x_ref, out_ref, nw_ref, mod_ref, 1.0, inv_ref, rows_ref)


def _cm_out(trunk, u, c, layer, x, mod3, nw):
    tm = 256
    nblk = trunk.tokens // tm
    blocks_per_seq = trunk.seq_len // tm
    hb = tm // CM_HALO
    last = trunk.tokens // CM_HALO - 1
    n_strips = D // CM_STRIP
    span = tm + (CM_HALO - CM_KERNEL // 2 + CM_KERNEL - 1) // SUBLANES * SUBLANES
    row = pl.BlockSpec((None, 1, D), lambda i: (layer, 0, 0))
    return pl.pallas_call(
        functools.partial(_cm_out_kernel, blocks_per_seq=blocks_per_seq),
        grid=(nblk,),
        in_specs=[
            pl.BlockSpec((tm, D), lambda i: (i, 0)),
            pl.BlockSpec((CM_HALO, D), lambda i: (jnp.maximum(i * hb - 1, 0), 0)),
            pl.BlockSpec((CM_HALO, D), lambda i: (jnp.minimum(i * hb + hb, last), 0)),
            pl.BlockSpec((None, n_strips, CM_KERNEL, SUBLANES, CM_STRIP), lambda i: (layer, 0, 0, 0, 0)),
            pl.BlockSpec((None, n_strips, SUBLANES, CM_STRIP), lambda i: (layer, 0, 0, 0)),
            row, row,
            pl.BlockSpec((None, D, D), lambda i: (layer, 0, 0)),
            pl.BlockSpec((tm, D), lambda i: (i, 0)),
            trunk.mod_spec(tm),
            pl.BlockSpec((2, D), lambda i: (0, 0)),
        ],
        out_specs=pl.BlockSpec((tm, D), lambda i: (i, 0)),
        out_shape=jax.ShapeDtypeStruct((trunk.tokens, D), F32),
        scratch_shapes=[pltpu.VMEM((n_strips, tm + 2 * CM_HALO, CM_STRIP), F32),
                        pltpu.VMEM((SUBLANES - 1, span, CM_STRIP), F32),
                        pltpu.VMEM((n_strips, tm, CM_STRIP), F32),
                        pltpu.VMEM((tm, D), BF16),
                        pltpu.VMEM((tm, D), F32),
                        pltpu.VMEM((tm, LANES), F32), pltpu.VMEM((2 * NORM_ROWS, D), F32)],
        compiler_params=_params(("arbitrary",), 48),
    )(u, u, u, c["dw_w"], c["dw_b"], c["ln_w"], c["ln_b"], c["pw2"], x, mod3, nw)


def _strip_rows(v, strip):
    v = v.reshape(v.shape[:-1] + (v.shape[-1] // strip, 1, strip))
    return jnp.broadcast_to(v, v.shape[:-2] + (SUBLANES, strip))


def _run_trunk(trunk, x, mod_t, h0_all, grid_rows, w, ffn_bf16, ffn_f32=None):
    state_buf = None

    def ffn(x, i, half, mod3, nw):
        nxt = (i, 1) if half == 0 else (i + 1, 0)
        cast = None
        if ffn_f32 is not None and nxt[0] < DEPTH and nxt not in ffn_bf16:
            cast = (*ffn_f32, *nxt)
        x, made = _ffn(trunk, x, mod3, nw, ffn_bf16[(i, half)], cast)
        if made is not None:
            ffn_bf16[nxt] = made
        return x

    for i in range(DEPTH):
        kind, j = i % 3, i // 3
        x = ffn(x, i, 0, mod_t[i, 0:3], w["norm_w"][i, 0])
        mod3, nw = mod_t[i, 3:6], w["norm_w"][i, 1]
        if kind == 0:
            x = _pool_mixer(trunk, x, mod3, nw, w["pool_w"], w["pool_scale"], j, grid_rows)
        elif kind == 1:
            h0 = None if h0_all is None else h0_all[:, j].reshape(trunk.nseq, 2, SSD_D_INNER, SSD_STATE)
            x, state_buf = _ssd_mixer(trunk, x, mod3, nw, h0, w["ssd"], j, state_buf)
        else:
            u = _cm_in(trunk, x, mod3, nw, w["cm"]["pw1"], j)
            x = _cm_out(trunk, u, w["cm"], j, x, mod3, nw)
        x = ffn(x, i, 1, mod_t[i, 6:9], w["norm_w"][i, 2])
    return x, state_buf


def kernel(x_prompt, x_sample, state_ssd, c, c_ctx, norm_w, ada_w, ada_b, ffn_wg, ffn_wu, ffn_wd, pool_w, pool_scale, ssd_in_w, ssd_conv_w, ssd_conv_b, ssd_a_log, ssd_dt_bias, ssd_d, ssd_norm_w, ssd_out_w, cm_pw1, cm_dw_w, cm_dw_b, cm_ln_w, cm_ln_b, cm_pw2):
    n_ctx, l_ctx, _ = x_prompt.shape
    n_lat, l_lat, _ = x_sample.shape
    ctx = _Trunk(n_ctx, l_ctx, shared_cond=True)
    lat = _Trunk(n_lat, l_lat, shared_cond=False)

    cond8 = jnp.zeros((8, D), F32).at[0].set(c_ctx).at[1:1 + n_lat].set(c)
    mod = _ada_mod(cond8, ada_w, ada_b).reshape(DEPTH, 8, N_MOD, D)
    mod = jnp.transpose(mod, (0, 2, 1, 3))[:, :, :, None, :]
    mod_ctx, mod_lat = mod[:, :, 0:1], mod[:, :, 1:1 + n_lat]

    n_ssd = ssd_in_w.shape[0]
    n_cm = cm_pw1.shape[0]
    ndt = 2 * SSD_HEADS
    w = {
        "norm_w": norm_w,
        "pool_w": pool_w.astype(BF16),
        "pool_scale": pool_scale[:, None, :],
        "ssd": dict(
            in_w=ssd_in_w.astype(BF16),
            conv_w=_strip_rows(ssd_conv_w, SSD_STRIP).transpose(0, 2, 1, 3, 4),
            conv_b=_strip_rows(ssd_conv_b, SSD_STRIP),
            dt_bias=ssd_dt_bias.reshape(n_ssd, 1, ndt), a_log=ssd_a_log.reshape(n_ssd, 1, ndt),
            d_skip=jnp.repeat(ssd_d, SSD_HEAD_DIM, axis=-1).reshape(n_ssd, SSD_GROUPS, 1, SSD_GROUP_DIM),
            gnorm=ssd_norm_w[:, None, :], out_w=ssd_out_w.astype(BF16)),
        "cm": dict(
            pw1=cm_pw1.astype(BF16),
            dw_w=_strip_rows(cm_dw_w, CM_STRIP).transpose(0, 2, 1, 3, 4),
            dw_b=_strip_rows(cm_dw_b, CM_STRIP),
            ln_w=cm_ln_w[:, None, :], ln_b=cm_ln_b[:, None, :], pw2=cm_pw2.astype(BF16)),
    }

    ffn_bf16 = {(0, 0): (ffn_wg[0, 0].astype(BF16), ffn_wu[0, 0].astype(BF16), ffn_wd[0, 0].astype(BF16))}
    y_ctx, new_state = _run_trunk(ctx, x_prompt.reshape(ctx.tokens, D), mod_ctx, None, None, w,
                                  ffn_bf16, (ffn_wg, ffn_wu, ffn_wd))
    y_lat, _ = _run_trunk(lat, x_sample.reshape(lat.tokens, D), mod_lat, state_ssd,
                          l_lat // GRID_W, w, ffn_bf16)

    new_state = new_state.reshape(n_ctx, n_ssd, 2, SSD_HEADS, SSD_HEAD_DIM, SSD_STATE)
    return (y_ctx.reshape(x_prompt.shape), y_lat.reshape(x_sample.shape),
            new_state.astype(x_prompt.dtype))
```

```python
import functools

import numpy as np
import jax
import jax.numpy as jnp
from jax import lax
from jax.experimental import pallas as pl
from jax.experimental.pallas import tpu as pltpu

F32 = jnp.float32
BF16 = jnp.bfloat16

D = 2048
DEPTH = 4
N_MOD = 9
D_FF = 5632
EPS = 1e-6
GRID_W = 64
POOL_WINDOWS = (2, 4, 8, 16)
POOL_GROUP_DIM = D // 4
SSD_D_INNER = 2 * D
SSD_HEAD_DIM = 64
SSD_HEADS = SSD_D_INNER // SSD_HEAD_DIM
SSD_GROUPS = 8
SSD_HEADS_PER_GROUP = SSD_HEADS // SSD_GROUPS
SSD_GROUP_DIM = SSD_HEADS_PER_GROUP * SSD_HEAD_DIM
SSD_STATE = 128
SSD_CONV = 5
SSD_CHUNK = 128
SSD_BC_DIM = SSD_GROUPS * SSD_STATE
SSD_ZX_DIM = 2 * SSD_D_INNER + 2 * SSD_BC_DIM
SSD_HALO = 8
SSD_STRIP = SSD_GROUP_DIM
SSD_X_STRIPS = SSD_D_INNER // SSD_STRIP
SSD_BC_STRIPS = 2 * SSD_BC_DIM // SSD_STRIP
CM_KERNEL = 31
CM_HALO = 16
CM_STRIP = 512
POOL_CHUNK = 256
SUBLANES = 8
LANES = 128
NORM_ROWS = 16

MIB = 1024 * 1024


def _silu(x):
    return x * (1.0 / (1.0 + jnp.exp(-x)))


def _sigmoid(x):
    return 1.0 / (1.0 + jnp.exp(-x))


def _rms(x, w):
    return x * lax.rsqrt(jnp.mean(x * x, axis=-1, keepdims=True) + EPS) * w


def _row_loop(n_rows, rows, body):
    def step(i, carry):
        body(pl.ds(pl.multiple_of(i * rows, rows), rows))
        return carry

    lax.fori_loop(0, n_rows // rows, step, 0, unroll=2)


def _inv_rms_to(src_ref, inv_ref):
    x = src_ref[...]
    inv = lax.rsqrt(jnp.mean(x * x, axis=-1, keepdims=True) + EPS)
    inv_ref[...] = jnp.broadcast_to(inv, inv_ref.shape)


def _inv_rows(inv_ref, r, width):
    return jnp.concatenate([inv_ref[r, :]] * (width // LANES), axis=1)


def _pre_rows(x_ref, nw_ref, mod_ref, h_ref, inv_ref, rows_ref):
    _inv_rms_to(x_ref, inv_ref)
    width = x_ref.shape[1]
    rows_ref[0:NORM_ROWS, :] = jnp.broadcast_to(nw_ref[0:1, :] * (1.0 + mod_ref[1, 0]),
                                                (NORM_ROWS, width))
    rows_ref[NORM_ROWS:2 * NORM_ROWS, :] = jnp.broadcast_to(mod_ref[0, 0], (NORM_ROWS, width))

    def body(r):
        y = x_ref[r, :] * _inv_rows(inv_ref, r, width) * rows_ref[0:NORM_ROWS, :]
        h_ref[r, :] = (y + rows_ref[NORM_ROWS:2 * NORM_ROWS, :]).astype(h_ref.dtype)

    _row_loop(x_ref.shape[0], NORM_ROWS, body)


def _post_rows(o_ref, x_ref, src_ref, nw_ref, mod_ref, res_w, inv_ref, rows_ref):
    _inv_rms_to(src_ref, inv_ref)
    width = x_ref.shape[1]
    gate = mod_ref[2, 0] if res_w == 1.0 else res_w * mod_ref[2, 0]
    rows_ref[0:NORM_ROWS, :] = jnp.broadcast_to(gate * nw_ref[1:2, :], (NORM_ROWS, width))

    def body(r):
        y = src_ref[r, :] * _inv_rows(inv_ref, r, width) * rows_ref[0:NORM_ROWS, :]
        o_ref[r, :] = x_ref[r, :] + y

    _row_loop(x_ref.shape[0], NORM_ROWS, body)


def _dot(a, b):
    return jnp.dot(a, b, preferred_element_type=F32)


def _dot_nt(a, b):
    return lax.dot_general(a, b, (((1,), (1,)), ((), ())), preferred_element_type=F32)


def _dot_tn(a, b):
    return lax.dot_general(a, b, (((0,), (0,)), ((), ())), preferred_element_type=F32)


def _split_bf16(v):
    hi = v.astype(BF16)
    lo = (v - hi.astype(F32)).astype(BF16)
    return hi, lo


def _params(semantics, vmem_mib):
    return pltpu.CompilerParams(dimension_semantics=semantics,
                                vmem_limit_bytes=vmem_mib * MIB)


class _Trunk:
    def __init__(self, nseq, seq_len, shared_cond):
        self.nseq = nseq
        self.seq_len = seq_len
        self.tokens = nseq * seq_len
        self.shared_cond = shared_cond

    def cond_row(self, tm):
        if self.shared_cond:
            return lambda i: 0
        per = self.seq_len // tm
        return lambda i: i // per

    def mod_spec(self, tm):
        row = self.cond_row(tm)
        return pl.BlockSpec((3, 1, 1, D), lambda i, *_: (0, row(i), 0, 0))


def _ada_kernel(c_ref, w_ref, b_ref, o_ref):
    sc = _silu(c_ref[...]).astype(BF16)
    o_ref[0] = _dot(sc, w_ref[0].astype(BF16)) + b_ref[0]


def _ada_mod(cond8, ada_w, ada_b):
    tn = 1024
    n = N_MOD * D
    return pl.pallas_call(
        _ada_kernel,
        grid=(DEPTH, n // tn),
        in_specs=[
            pl.BlockSpec((8, D), lambda l, j: (0, 0)),
            pl.BlockSpec((1, D, tn), lambda l, j: (l, 0, j)),
            pl.BlockSpec((1, 1, tn), lambda l, j: (l, 0, j)),
        ],
        out_specs=pl.BlockSpec((1, 8, tn), lambda l, j: (l, 0, j)),
        out_shape=jax.ShapeDtypeStruct((DEPTH, 8, n), F32),
        compiler_params=_params(("arbitrary", "arbitrary"), 52),
    )(cond8, ada_w, ada_b.reshape(DEPTH, 1, n))


def _ffn_kernel(x_ref, mod_ref, nw_ref, wg_ref, wu_ref, wd_ref, *rest, cast_next):
    if cast_next:
        srcs, rest = rest[:3], rest[3:]
        o_ref, dsts, rest = rest[0], rest[1:4], rest[4:]
        for src, dst in zip(srcs, dsts):
            dst[...] = src[...].astype(BF16)
    else:
        o_ref, rest = rest[0], rest[1:]
    h_ref, inv_ref = rest
    j = pl.program_id(1)
    last = pl.num_programs(1) - 1
    tm = x_ref.shape[0]

    def ff_rows(h, rows, assign):
        a = (_silu(_dot(h, wg_ref[...])) * _dot(h, wu_ref[...])).astype(BF16)
        d = _dot(a, wd_ref[...])
        if assign:
            o_ref[rows, :] = d
        else:
            o_ref[rows, :] += d

    @pl.when(j == 0)
    def _():
        _inv_rms_to(x_ref, inv_ref)
        inv = jnp.concatenate([inv_ref[...]] * (D // LANES), axis=1)
        h = x_ref[...] * inv * (nw_ref[0:1, :] * (1.0 + mod_ref[1, 0])) + mod_ref[0, 0]
        h = h.astype(BF16)
        h_ref[...] = h
        ff_rows(h, slice(0, tm), True)

    @pl.when((j > 0) & (j < last))
    def _():
        ff_rows(h_ref[...], slice(0, tm), False)

    @pl.when(j == last)
    def _():
        half = tm // 2
        for rows in (slice(0, half), slice(half, tm)):
            ff_rows(h_ref[rows, :], rows, False)
            acc = o_ref[rows, :]
            inv = lax.rsqrt(jnp.mean(acc * acc, axis=-1, keepdims=True) + EPS)
            o_ref[rows, :] = x_ref[rows, :] + acc * inv * ((0.5 * mod_ref[2, 0]) * nw_ref[1:2, :])


def _ffn(trunk, x, mod3, nw, weights, cast_next=None):
    tm, tf, vmem_mib = (1024, 512, 58) if cast_next is None else (512, 512, 48)
    nblk = trunk.tokens // tm
    in_specs = [
        pl.BlockSpec((tm, D), lambda i, j: (i, 0)),
        trunk.mod_spec(tm),
        pl.BlockSpec((2, D), lambda i, j: (0, 0)),
        pl.BlockSpec((D, tf), lambda i, j: (0, j)),
        pl.BlockSpec((D, tf), lambda i, j: (0, j)),
        pl.BlockSpec((tf, D), lambda i, j: (j, 0)),
    ]
    inputs = [x, mod3, nw, *weights]
    out_specs = [pl.BlockSpec((tm, D), lambda i, j: (i, 0))]
    out_shape = [jax.ShapeDtypeStruct((trunk.tokens, D), F32)]
    if cast_next is not None:
        fg, fu, fd, layer, half = cast_next
        td = D // nblk
        in_specs += [
            pl.BlockSpec((None, None, td, tf), lambda i, j: (layer, half, i, j)),
            pl.BlockSpec((None, None, td, tf), lambda i, j: (layer, half, i, j)),
            pl.BlockSpec((None, None, tf, td), lambda i, j: (layer, half, j, i)),
        ]
        inputs += [fg, fu, fd]
        out_specs += [pl.BlockSpec((td, tf), lambda i, j: (i, j)),
                      pl.BlockSpec((td, tf), lambda i, j: (i, j)),
                      pl.BlockSpec((tf, td), lambda i, j: (j, i))]
        out_shape += [jax.ShapeDtypeStruct((D, D_FF), BF16), jax.ShapeDtypeStruct((D, D_FF), BF16),
                      jax.ShapeDtypeStruct((D_FF, D), BF16)]
    res = pl.pallas_call(
        functools.partial(_ffn_kernel, cast_next=cast_next is not None),
        grid=(nblk, D_FF // tf),
        in_specs=in_specs, out_specs=out_specs, out_shape=out_shape,
        scratch_shapes=[pltpu.VMEM((tm, D), BF16), pltpu.VMEM((tm, LANES), F32)],
        compiler_params=_params(("arbitrary", "arbitrary"), vmem_mib),
    )(*inputs)
    return res[0], (tuple(res[1:]) if cast_next is not None else None)


def _window_bounds(n, w):
    pos = np.arange(n)
    lo = np.clip(pos - w // 2, 0, n)
    hi = np.clip(pos + (w - w // 2), 0, n)
    return lo, hi


def _band_matrix(n, w):
    lo, hi = _window_bounds(n, w)
    col = np.arange(n)[None, :]
    return ((col >= lo[:, None]) & (col < hi[:, None])).astype(np.float32)


def _window_sum(a, v):
    hi, lo = _split_bf16(v)
    return _dot(a, hi) + _dot(a, lo)


def _inv_rms(x_ref):
    x = x_ref[...]
    return lax.rsqrt(jnp.mean(x * x, axis=-1, keepdims=True) + EPS)


def _pre_slice(x_ref, inv, nw_ref, mod_ref, rows, cols):
    y = x_ref[rows, cols] * inv[rows, :] * nw_ref[0:1, cols]
    return y * (1.0 + mod_ref[1, 0, :, cols]) + mod_ref[0, 0, :, cols]


def _pool_ctx_kernel(x_ref, mod_ref, nw_ref, a_ref, ic_ref, pw_ref, ps_ref, o_ref, mix_ref, inv_ref,
                     rows_ref, *, tm):
    inv = _inv_rms(x_ref)
    gd = POOL_GROUP_DIM
    for g in range(len(POOL_WINDOWS)):
        cols = slice(g * gd, (g + 1) * gd)
        a = a_ref[g]
        for q in range(tm // POOL_CHUNK):
            rows = slice(q * POOL_CHUNK, (q + 1) * POOL_CHUNK)
            v = _pre_slice(x_ref, inv, nw_ref, mod_ref, rows, cols)
            d = (_window_sum(a, v) * ic_ref[g] - v).astype(BF16)
            mix_ref[rows, cols] = _dot(d, pw_ref[g]) * ps_ref[:, cols]
    _post_rows(o_ref, x_ref, mix_ref, nw_ref, mod_ref, 1.0, inv_ref, rows_ref)


def _pool_lat_kernel(xp_ref, xc_ref, xn_ref, mod_ref, nw_ref, a_ref, ic_ref, pw_ref, ps_ref,
                     o_ref, mix_ref, inv_ref, rows_ref, *, blocks_per_seq):
    jb = pl.program_id(0) % blocks_per_seq
    valid_p = jb > 0
    valid_n = jb < blocks_per_seq - 1
    x_refs = (xp_ref, xc_ref, xn_ref)
    invs = tuple(_inv_rms(r) for r in x_refs)
    gd = POOL_GROUP_DIM
    rows_per_chunk = POOL_CHUNK // GRID_W
    rows_per_block = xc_ref.shape[0] // GRID_W
    chunks_per_block = rows_per_block // rows_per_chunk
    for g, w in enumerate(POOL_WINDOWS):
        before, after = w // 2, w - w // 2 - 1
        cols = slice(g * gd, (g + 1) * gd)
        a = a_ref[g]
        first_row = rows_per_block - before
        last_row = 2 * rows_per_block - 1 + after
        cm = {}
        for ch in range(first_row // rows_per_chunk, last_row // rows_per_chunk + 1):
            which = ch // chunks_per_block
            off = (ch % chunks_per_block) * POOL_CHUNK
            v = _pre_slice(x_refs[which], invs[which], nw_ref, mod_ref,
                           slice(off, off + POOL_CHUNK), cols)
            s = _window_sum(a, v)
            if ch < chunks_per_block:
                s = jnp.where(valid_p, s, 0.0)
            elif ch >= 2 * chunks_per_block:
                s = jnp.where(valid_n, s, 0.0)
            cm[ch] = s

        def slab(r):
            o = (r % rows_per_chunk) * GRID_W
            return cm[r // rows_per_chunk][o:o + GRID_W, :]

        outs = []
        for r in range(rows_per_block, 2 * rows_per_block):
            acc = slab(r - before)
            for k in range(-before + 1, after + 1):
                acc = acc + slab(r + k)
            outs.append(acc)
        m = jnp.concatenate(outs, axis=0) * ic_ref[0, g]
        v = _pre_slice(xc_ref, invs[1], nw_ref, mod_ref, slice(None), cols)
        d = (m - v).astype(BF16)
        mix_ref[:, cols] = _dot(d, pw_ref[g]) * ps_ref[:, cols]
    _post_rows(o_ref, xc_ref, mix_ref, nw_ref, mod_ref, 1.0, inv_ref, rows_ref)


def _pool_ctx_consts(seq_len):
    mats, inv = [], []
    for w in POOL_WINDOWS:
        lo, hi = _window_bounds(seq_len, w)
        mats.append(_band_matrix(seq_len, w))
        inv.append((1.0 / (hi - lo)).astype(np.float32)[:, None])
    return jnp.asarray(np.stack(mats), BF16), jnp.asarray(np.stack(inv), F32)


def _pool_lat_consts(rows, tm):
    rows_per_block = tm // GRID_W
    mats, inv = [], []
    for w in POOL_WINDOWS:
        mats.append(np.kron(np.eye(POOL_CHUNK // GRID_W, dtype=np.float32), _band_matrix(GRID_W, w)))
        lo_c, hi_c = _window_bounds(GRID_W, w)
        lo_r, hi_r = _window_bounds(rows, w)
        cnt = (hi_r - lo_r)[:, None] * (hi_c - lo_c)[None, :]
        inv.append((1.0 / cnt).astype(np.float32).reshape(rows // rows_per_block, tm, 1))
    inv = np.stack(inv, axis=1)
    return jnp.asarray(np.stack(mats), BF16), jnp.asarray(inv, F32)


def _pool_mixer(trunk, x, mod3, nw, pool_w, pool_scale, layer, grid_rows):
    tm = 512
    nblk = trunk.tokens // tm
    gd = POOL_GROUP_DIM
    common_specs = [
        trunk.mod_spec(tm),
        pl.BlockSpec((2, D), lambda i: (0, 0)),
        pl.BlockSpec((4, POOL_CHUNK, POOL_CHUNK), lambda i: (0, 0, 0)),
    ]
    tail_specs = [
        pl.BlockSpec((None, 4, gd, gd), lambda i: (layer, 0, 0, 0)),
        pl.BlockSpec((None, 1, D), lambda i: (layer, 0, 0)),
    ]
    out_spec = pl.BlockSpec((tm, D), lambda i: (i, 0))
    out_shape = jax.ShapeDtypeStruct((trunk.tokens, D), F32)
    scratch = [pltpu.VMEM((tm, D), F32), pltpu.VMEM((tm, LANES), F32), pltpu.VMEM((2 * NORM_ROWS, D), F32)]
    if grid_rows is None:
        assert trunk.seq_len == POOL_CHUNK
        a, ic = _pool_ctx_consts(trunk.seq_len)
        return pl.pallas_call(
            functools.partial(_pool_ctx_kernel, tm=tm),
            grid=(nblk,),
            in_specs=[pl.BlockSpec((tm, D), lambda i: (i, 0))] + common_specs
            + [pl.BlockSpec((4, POOL_CHUNK, 1), lambda i: (0, 0, 0))] + tail_specs,
            out_specs=out_spec, out_shape=out_shape, scratch_shapes=scratch,
            compiler_params=_params(("arbitrary",), 48),
        )(x, mod3, nw, a, ic, pool_w, pool_scale)
    blocks_per_seq = trunk.seq_len // tm
    assert max(POOL_WINDOWS) // 2 <= tm // GRID_W
    a, ic = _pool_lat_consts(grid_rows, tm)
    return pl.pallas_call(
        functools.partial(_pool_lat_kernel, blocks_per_seq=blocks_per_seq),
        grid=(nblk,),
        in_specs=[
            pl.BlockSpec((tm, D), lambda i: (jnp.maximum(i - 1, 0), 0)),
            pl.BlockSpec((tm, D), lambda i: (i, 0)),
            pl.BlockSpec((tm, D), lambda i: (jnp.minimum(i + 1, nblk - 1), 0)),
        ] + common_specs
        + [pl.BlockSpec((1, 4, tm, 1), lambda i: (i % blocks_per_seq, 0, 0, 0))] + tail_specs,
        out_specs=out_spec, out_shape=out_shape, scratch_shapes=scratch,
        compiler_params=_params(("arbitrary",), 56),
    )(x, x, x, mod3, nw, a, ic, pool_w, pool_scale)


def _ssd_in_kernel(x_ref, mod_ref, nw_ref, w_ref, wdt_ref, zx_ref, dt_ref, h_ref, inv_ref, rows_ref):
    @pl.when(pl.program_id(1) == 0)
    def _():
        _pre_rows(x_ref, nw_ref, mod_ref, h_ref, inv_ref, rows_ref)
        dt_ref[...] = _dot(h_ref[...], wdt_ref[...])

    zx_ref[...] = _dot(h_ref[...], w_ref[...])


def _ssd_in(trunk, x, mod3, nw, in_w, layer):
    tm, tn = 512, 2048
    ndt = 2 * SSD_HEADS
    return pl.pallas_call(
        _ssd_in_kernel,
        grid=(trunk.tokens // tm, SSD_ZX_DIM // tn),
        in_specs=[
            pl.BlockSpec((tm, D), lambda i, j: (i, 0)),
            trunk.mod_spec(tm),
            pl.BlockSpec((2, D), lambda i, j: (0, 0)),
            pl.BlockSpec((None, D, tn), lambda i, j: (layer, 0, j)),
            pl.BlockSpec((None, D, ndt), lambda i, j: (layer, 0, SSD_ZX_DIM // ndt)),
        ],
        out_specs=[
            pl.BlockSpec((tm, tn), lambda i, j: (i, j)),
            pl.BlockSpec((tm, ndt), lambda i, j: (i, 0)),
        ],
        out_shape=[
            jax.ShapeDtypeStruct((trunk.tokens, SSD_ZX_DIM), F32),
            jax.ShapeDtypeStruct((trunk.tokens, ndt), F32),
        ],
        scratch_shapes=[pltpu.VMEM((tm, D), BF16), pltpu.VMEM((tm, LANES), F32), pltpu.VMEM((2 * NORM_ROWS, D), F32)],
        compiler_params=_params(("arbitrary", "arbitrary"), 52),
    )(x, mod3, nw, in_w, in_w)


def _ssd_conv_strips(buf_ref, w_ref, b_ref, first, count, store):
    base = SSD_HALO - SSD_CONV // 2
    rb = 4 * SUBLANES

    def strip(s, carry):
        for r in range(0, SSD_CHUNK, rb):
            acc = b_ref[s][None]
            for t in range(SSD_CONV):
                x = buf_ref[s, base + t + r:base + t + r + rb, :]
                acc = acc + w_ref[s, t][None] * x.reshape(rb // SUBLANES, SUBLANES, SSD_STRIP)
            store(s, r, _silu(acc).reshape(rb, SSD_STRIP))
        return carry

    for s in range(first, first + count):
        strip(s, 0)


def _ssd_scan_kernel(*refs, reverse, nc, has_h0, emit_state, alias_state):
    it = iter(refs)
    if not reverse:
        raw = [(next(it), next(it), next(it)) for _ in range(3)]
        cw_ref, cbias_ref = next(it), next(it)
    else:
        xsc_ref, bcc_ref = next(it), next(it)
    dt_ref, dtb_ref, alog_ref, e_ref = next(it), next(it), next(it), next(it)
    h0_ref = next(it) if has_h0 else None
    if reverse:
        yf_ref, z_ref, gn_ref = next(it), next(it), next(it)
    else:
        dskip_ref = next(it)
    if alias_state:
        next(it)
    if reverse:
        yn_ref = next(it)
    else:
        yf_ref, xsc_ref, bcc_ref = next(it), next(it), next(it)
    so_ref = next(it) if emit_state else None
    state_ref, acol_s, arow_s, dtrow_s, lhs_off_s, lhs_w_s = (next(it) for _ in range(6))
    if reverse:
        y_s, ytmp_ref, inv_ref = next(it), next(it), next(it)
    else:
        xbuf_ref, bct_ref = next(it), next(it)

    t = SSD_CHUNK
    step = pl.program_id(1)
    cc = (nc - 1 - step) if reverse else step
    gdim = SSD_GROUP_DIM

    @pl.when(step == 0)
    def _():
        if has_h0:
            for g in range(SSD_GROUPS):
                for k in range(gdim // t):
                    r0 = g * gdim + k * t
                    state_ref[g, :, k * t:(k + 1) * t] = h0_ref[0, 0, r0:r0 + t, :].T
        else:
            state_ref[...] = jnp.zeros_like(state_ref)

    if not reverse:
        has_prev = cc > 0
        has_next = cc < nc - 1
        s0 = 0
        for cur, prev, nxt in raw:
            for k in range(cur.shape[1] // SSD_STRIP):
                lanes = slice(k * SSD_STRIP, (k + 1) * SSD_STRIP)
                xbuf_ref[s0 + k, 0:SSD_HALO, :] = jnp.where(has_prev, prev[:, lanes], 0.0)
                xbuf_ref[s0 + k, SSD_HALO:SSD_HALO + t, :] = cur[:, lanes]
                xbuf_ref[s0 + k, SSD_HALO + t:SSD_HALO + t + SSD_HALO, :] = jnp.where(
                    has_next, nxt[:, lanes], 0.0)
            s0 += cur.shape[1] // SSD_STRIP

        def store_x(s, r, v):
            xsc_ref[0, s, r:r + v.shape[0], :] = v

        def store_bc(s, r, v):
            bct_ref[s - SSD_X_STRIPS, r:r + v.shape[0], :] = v

        _ssd_conv_strips(xbuf_ref, cw_ref, cbias_ref, 0, SSD_X_STRIPS, store_x)
        _ssd_conv_strips(xbuf_ref, cw_ref, cbias_ref, SSD_X_STRIPS, SSD_BC_STRIPS, store_bc)
        per = SSD_STRIP // SSD_STATE
        for k in range(2 * SSD_GROUPS):
            bcc_ref[0, k] = bct_ref[k // per, :, (k % per) * SSD_STATE:(k % per + 1) * SSD_STATE].astype(BF16)

    dt_raw = dt_ref[...] + dtb_ref[...]
    dt = jnp.maximum(dt_raw, 0.0) + jnp.log1p(jnp.exp(-jnp.abs(dt_raw)))
    dta = dt * (-jnp.exp(alog_ref[...]))

    row = lax.broadcasted_iota(jnp.int32, (t, t), 0)
    col = lax.broadcasted_iota(jnp.int32, (t, t), 1)
    mask = (col >= row) if reverse else (col <= row)
    hi = dta.astype(BF16)
    mid, lo = _split_bf16(dta - hi.astype(F32))
    tri = jnp.where(mask, 1.0, 0.0).astype(BF16)
    parts = _dot(tri, jnp.concatenate([hi, mid, lo], axis=1))
    acum = parts[:, 0:t] + parts[:, t:2 * t] + parts[:, 2 * t:3 * t]
    acum_t = acum.T
    dt_t = dt.T
    end = 0 if reverse else t - 1
    d0 = SSD_HEADS if reverse else 0
    mine = (col >= d0) & (col < d0 + SSD_HEADS)

    hi, lo = _split_bf16(jnp.where(mine, jnp.exp(acum), 0.0))
    lhs_off_s[...] = jnp.concatenate([hi, lo], axis=1)
    hi, lo = _split_bf16(jnp.where(mine, dt * jnp.exp(acum[end:end + 1, :] - acum), 0.0))
    lhs_w_s[...] = jnp.concatenate([hi, lo], axis=1)

    hpg = SSD_HEADS_PER_GROUP
    for g in range(SSD_GROUPS):
        heads = slice(d0 + g * hpg, d0 + (g + 1) * hpg)
        acol_s[g] = acum[:, heads]
        arow_s[g] = acum_t[heads, :]
        dtrow_s[g] = dt_t[heads, :]

    pair = 2 * SSD_HEAD_DIM
    first_head = lax.broadcasted_iota(jnp.int32, (t, pair), 1) < SSD_HEAD_DIM

    def group_body(g, carry):
        xg = xsc_ref[0, g]
        bg = bcc_ref[0, g]
        cg = bcc_ref[0, SSD_GROUPS + g]
        acol = acol_s[g]
        arow = arow_s[g]
        dtrow = dtrow_s[g]
        off = _dot(lhs_off_s[...], e_ref[g])
        wgt = _dot(lhs_w_s[...], e_ref[g])
        cb = _dot_nt(cg, bg)
        st = state_ref[g]
        y_off = _dot(cg, st.astype(BF16))
        new = _dot_tn(bg, (xg * wgt).astype(BF16))
        state_ref[g] = st * off[end:end + 1, :] + new
        ys = []
        for p in range(hpg // 2):
            lanes = slice(p * pair, (p + 1) * pair)
            sc = []
            for r in (2 * p, 2 * p + 1):
                lm = jnp.exp(jnp.where(mask, acol[:, r:r + 1] - arow[r:r + 1, :], -jnp.inf))
                sc.append((cb * lm * dtrow[r:r + 1, :]).astype(BF16))
            xp = xg[:, lanes]
            x_bd = jnp.concatenate([jnp.where(first_head, xp, 0.0),
                                    jnp.where(first_head, 0.0, xp)], axis=0).astype(BF16)
            ys.append(_dot(jnp.concatenate(sc, axis=1), x_bd) + y_off[:, lanes] * off[:, lanes])
        y = jnp.concatenate(ys, axis=1)
        if reverse:
            y_s[g] = y
        else:
            yf_ref[0, g] = y + dskip_ref[g] * xg
        return carry

    for g in range(SSD_GROUPS):
        group_body(g, 0)

    if reverse:
        for g in range(SSD_GROUPS):
            lanes = slice(g * gdim, (g + 1) * gdim)
            ytmp_ref[:, lanes] = (yf_ref[0, g] + y_s[g]) * _silu(z_ref[:, lanes])
        _inv_rms_to(ytmp_ref, inv_ref)

        def norm(r):
            y = ytmp_ref[r, :] * _inv_rows(inv_ref, r, SSD_D_INNER) * gn_ref[...]
            yn_ref[r, :] = y.astype(yn_ref.dtype)

        _row_loop(t, NORM_ROWS, norm)

    if emit_state:
        @pl.when(step == nc - 1)
        def _():
            for g in range(SSD_GROUPS):
                for k in range(gdim // t):
                    r0 = g * gdim + k * t
                    so_ref[0, 0, 0, r0:r0 + t, :] = state_ref[g, :, k * t:(k + 1) * t].T


def _ssd_expand_consts():
    e = np.zeros((2, SSD_GROUPS, 4 * SSD_HEADS, SSD_GROUP_DIM), np.float32)
    for d in range(2):
        for h in range(SSD_HEADS):
            g, r = divmod(h, SSD_HEADS_PER_GROUP)
            for half in range(2):
                e[d, g, half * 2 * SSD_HEADS + d * SSD_HEADS + h,
                  r * SSD_HEAD_DIM:(r + 1) * SSD_HEAD_DIM] = 1.0
    return jnp.asarray(e, BF16)


def _ssd_scan(trunk, p, layer, dt, *, reverse, zx=None, conv=None, y_fwd=None, h0=None,
              state_buf=None):
    t = SSD_CHUNK
    nc = trunk.seq_len // t
    nchunks = trunk.tokens // t
    emit_state = h0 is None
    alias_state = emit_state and state_buf is not None
    hb = t // SSD_HALO
    last8 = trunk.tokens // SSD_HALO - 1
    di, bc = SSD_D_INNER, SSD_BC_DIM
    n_strips = SSD_X_STRIPS + SSD_BC_STRIPS
    ndt = 2 * SSD_HEADS
    direction = 1 if reverse else 0

    def blk(b, c):
        return b * nc + ((nc - 1 - c) if reverse else c)

    def full(shape):
        return pl.BlockSpec(shape, lambda b, c: (0,) * len(shape))

    def chunk_major(lead, rows, lanes):
        return pl.BlockSpec((1, lead, rows, lanes), lambda b, c: (blk(b, c), 0, 0, 0))

    inputs, specs = [], []

    def add(arr, spec):
        inputs.append(arr)
        specs.append(spec)

    if not reverse:
        for width, colb in ((di, 1), (bc, 2 * di // bc), (bc, 2 * di // bc + 1)):
            add(zx, pl.BlockSpec((t, width), lambda b, c, colb=colb: (blk(b, c), colb)))
            add(zx, pl.BlockSpec((SSD_HALO, width),
                                 lambda b, c, colb=colb: (jnp.maximum(blk(b, c) * hb - 1, 0), colb)))
            add(zx, pl.BlockSpec((SSD_HALO, width),
                                 lambda b, c, colb=colb: (jnp.minimum(blk(b, c) * hb + hb, last8), colb)))
        add(p["conv_w"], pl.BlockSpec((None, n_strips, SSD_CONV, SUBLANES, SSD_STRIP),
                                      lambda b, c: (layer, 0, 0, 0, 0)))
        add(p["conv_b"], pl.BlockSpec((None, n_strips, SUBLANES, SSD_STRIP),
                                      lambda b, c: (layer, 0, 0, 0)))
    else:
        add(conv[0], chunk_major(SSD_GROUPS, t, SSD_GROUP_DIM))
        add(conv[1], chunk_major(2 * SSD_GROUPS, t, SSD_STATE))
    add(dt, pl.BlockSpec((t, ndt), lambda b, c: (blk(b, c), 0)))
    add(p["dt_bias"], pl.BlockSpec((None, 1, ndt), lambda b, c: (layer, 0, 0)))
    add(p["a_log"], pl.BlockSpec((None, 1, ndt), lambda b, c: (layer, 0, 0)))
    add(_ssd_expand_consts(), pl.BlockSpec((None, SSD_GROUPS, 2 * ndt, SSD_GROUP_DIM),
                                           lambda b, c: (direction, 0, 0, 0)))
    if h0 is not None:
        add(h0, pl.BlockSpec((1, 1, di, SSD_STATE), lambda b, c: (b, direction, 0, 0)))
    if reverse:
        add(y_fwd, chunk_major(SSD_GROUPS, t, SSD_GROUP_DIM))
        add(zx, pl.BlockSpec((t, di), lambda b, c: (blk(b, c), 0)))
        add(p["gnorm"], pl.BlockSpec((None, 1, di), lambda b, c: (layer, 0, 0)))
    else:
        add(p["d_skip"], pl.BlockSpec((None, SSD_GROUPS, 1, SSD_GROUP_DIM), lambda b, c: (layer, 0, 0, 0)))
    aliases = {}
    if alias_state:
        aliases[len(inputs)] = 1 if reverse else 3
        add(state_buf, pl.BlockSpec(memory_space=pl.ANY))

    if reverse:
        out_specs = [pl.BlockSpec((t, di), lambda b, c: (blk(b, c), 0))]
        out_shape = [jax.ShapeDtypeStruct((trunk.tokens, di), BF16)]
    else:
        out_specs = [chunk_major(SSD_GROUPS, t, SSD_GROUP_DIM),
                     chunk_major(SSD_GROUPS, t, SSD_GROUP_DIM),
                     chunk_major(2 * SSD_GROUPS, t, SSD_STATE)]
        out_shape = [jax.ShapeDtypeStruct((nchunks, SSD_GROUPS, t, SSD_GROUP_DIM), F32),
                     jax.ShapeDtypeStruct((nchunks, SSD_GROUPS, t, SSD_GROUP_DIM), F32),
                     jax.ShapeDtypeStruct((nchunks, 2 * SSD_GROUPS, t, SSD_STATE), BF16)]
    if emit_state:
        n_layers = p["a_log"].shape[0]
        out_specs.append(pl.BlockSpec((1, 1, 1, di, SSD_STATE),
                                      lambda b, c: (b, layer, direction, 0, 0)))
        out_shape.append(jax.ShapeDtypeStruct((trunk.nseq, n_layers, 2, di, SSD_STATE), F32))

    hpg = SSD_HEADS_PER_GROUP
    scratch = [
        pltpu.VMEM((SSD_GROUPS, SSD_STATE, SSD_GROUP_DIM), F32),
        pltpu.VMEM((SSD_GROUPS, t, hpg), F32),
        pltpu.VMEM((SSD_GROUPS, hpg, t), F32),
        pltpu.VMEM((SSD_GROUPS, hpg, t), F32),
        pltpu.VMEM((t, 2 * ndt), BF16),
        pltpu.VMEM((t, 2 * ndt), BF16),
    ]
    if reverse:
        scratch += [pltpu.VMEM((SSD_GROUPS, t, SSD_GROUP_DIM), F32), pltpu.VMEM((t, di), F32),
                    pltpu.VMEM((t, LANES), F32)]
    else:
        scratch += [pltpu.VMEM((n_strips, t + 2 * SSD_HALO, SSD_STRIP), F32),
                    pltpu.VMEM((SSD_BC_STRIPS, t, SSD_STRIP), F32)]
    res = pl.pallas_call(
        functools.partial(_ssd_scan_kernel, reverse=reverse, nc=nc, has_h0=h0 is not None,
                          emit_state=emit_state, alias_state=alias_state),
        grid=(trunk.nseq, nc),
        in_specs=specs, out_specs=out_specs, out_shape=out_shape, scratch_shapes=scratch,
        input_output_aliases=aliases,
        compiler_params=_params(("arbitrary", "arbitrary"), 48),
    )(*inputs)
    return res


def _mm_post_kernel(a_ref, w_ref, x_ref, mod_ref, nw_ref, o_ref, acc_ref, inv_ref, rows_ref):
    k = pl.program_id(1)

    @pl.when(k == 0)
    def _():
        acc_ref[...] = jnp.zeros_like(acc_ref)

    acc_ref[...] += _dot(a_ref[...], w_ref[...])

    @pl.when(k == pl.num_programs(1) - 1)
    def _():
        _post_rows(o_ref, x_ref, acc_ref, nw_ref, mod_ref, 1.0, inv_ref, rows_ref)


def _mm_post(trunk, a, w, layer, x, mod3, nw):
    tm, tk = 512, 2048
    kdim = a.shape[1]
    return pl.pallas_call(
        _mm_post_kernel,
        grid=(trunk.tokens // tm, kdim // tk),
        in_specs=[
            pl.BlockSpec((tm, tk), lambda i, k: (i, k)),
            pl.BlockSpec((None, tk, D), lambda i, k: (layer, k, 0)),
            pl.BlockSpec((tm, D), lambda i, k: (i, 0)),
            trunk.mod_spec(tm),
            pl.BlockSpec((2, D), lambda i, k: (0, 0)),
        ],
        out_specs=pl.BlockSpec((tm, D), lambda i, k: (i, 0)),
        out_shape=jax.ShapeDtypeStruct((trunk.tokens, D), F32),
        scratch_shapes=[pltpu.VMEM((tm, D), F32), pltpu.VMEM((tm, LANES), F32), pltpu.VMEM((2 * NORM_ROWS, D), F32)],
        compiler_params=_params(("arbitrary", "arbitrary"), 52),
    )(a, w, x, mod3, nw)


def _ssd_mixer(trunk, x, mod3, nw, h0, p, layer, state_buf):
    zx, dt = _ssd_in(trunk, x, mod3, nw, p["in_w"], layer)
    res = _ssd_scan(trunk, p, layer, dt, reverse=False, zx=zx, h0=h0, state_buf=state_buf)
    y_f, conv = res[0], (res[1], res[2])
    if h0 is None:
        state_buf = res[3]
    res = _ssd_scan(trunk, p, layer, dt, reverse=True, zx=zx, conv=conv, y_fwd=y_f, h0=h0,
                    state_buf=state_buf)
    if h0 is None:
        state_buf = res[1]
    return _mm_post(trunk, res[0], p["out_w"], layer, x, mod3, nw), state_buf


def _cm_in_kernel(x_ref, mod_ref, nw_ref, wa_ref, wb_ref, o_ref, h_ref, inv_ref, rows_ref):
    @pl.when(pl.program_id(1) == 0)
    def _():
        _pre_rows(x_ref, nw_ref, mod_ref, h_ref, inv_ref, rows_ref)

    h = h_ref[...]
    o_ref[...] = _dot(h, wa_ref[...]) * _sigmoid(_dot(h, wb_ref[...]))


def _cm_in(trunk, x, mod3, nw, pw1, layer):
    tm, tn = 512, 1024
    nb = D // tn
    return pl.pallas_call(
        _cm_in_kernel,
        grid=(trunk.tokens // tm, nb),
        in_specs=[
            pl.BlockSpec((tm, D), lambda i, j: (i, 0)),
            trunk.mod_spec(tm),
            pl.BlockSpec((2, D), lambda i, j: (0, 0)),
            pl.BlockSpec((None, D, tn), lambda i, j: (layer, 0, j)),
            pl.BlockSpec((None, D, tn), lambda i, j: (layer, 0, j + nb)),
        ],
        out_specs=pl.BlockSpec((tm, tn), lambda i, j: (i, j)),
        out_shape=jax.ShapeDtypeStruct((trunk.tokens, D), F32),
        scratch_shapes=[pltpu.VMEM((tm, D), BF16), pltpu.VMEM((tm, LANES), F32), pltpu.VMEM((2 * NORM_ROWS, D), F32)],
        compiler_params=_params(("arbitrary", "arbitrary"), 52),
    )(x, mod3, nw, pw1, pw1)


def _cm_out_kernel(u_ref, up_ref, un_ref, dw_ref, db_ref, lw_ref, lb_ref, w2_ref, x_ref, mod_ref,
                   nw_ref, o_ref, buf_ref, sh_ref, cv_ref, v_ref, out_ref, inv_ref, rows_ref, *,
                   blocks_per_seq):
    tm = u_ref.shape[0]
    jb = pl.program_id(0) % blocks_per_seq
    n_strips = D // CM_STRIP
    for s in range(n_strips):
        lanes = slice(s * CM_STRIP, (s + 1) * CM_STRIP)
        buf_ref[s, 0:CM_HALO, :] = jnp.where(jb > 0, up_ref[:, lanes], 0.0)
        buf_ref[s, CM_HALO:CM_HALO + tm, :] = u_ref[:, lanes]
        buf_ref[s, CM_HALO + tm:CM_HALO + tm + CM_HALO, :] = jnp.where(
            jb < blocks_per_seq - 1, un_ref[:, lanes], 0.0)

    base = CM_HALO - CM_KERNEL // 2
    span = tm + (base + CM_KERNEL - 1) // SUBLANES * SUBLANES
    rb = 4 * SUBLANES

    def strip(s, carry):
        for r in range(1, SUBLANES):
            sh_ref[r - 1] = buf_ref[s, r:r + span, :]
        for r0 in range(0, tm, rb):
            acc = db_ref[s][None]
            for t in range(CM_KERNEL):
                q, r = divmod(base + t, SUBLANES)
                lo = r0 + q * SUBLANES
                x = buf_ref[s, lo:lo + rb, :] if r == 0 else sh_ref[r - 1, lo:lo + rb, :]
                acc = acc + dw_ref[s, t][None] * x.reshape(rb // SUBLANES, SUBLANES, CM_STRIP)
            cv_ref[s, r0:r0 + rb, :] = acc.reshape(rb, CM_STRIP)
        return carry

    lax.fori_loop(0, n_strips, strip, 0)

    total = cv_ref[0]
    for s in range(1, n_strips):
        total = total + cv_ref[s]
    mu = jnp.sum(total, axis=-1, keepdims=True) * (1.0 / D)
    sq = None
    for s in range(n_strips):
        cen = cv_ref[s] - mu
        sq = cen * cen if sq is None else sq + cen * cen
    inv = lax.rsqrt(jnp.sum(sq, axis=-1, keepdims=True) * (1.0 / D) + EPS)
    for s in range(n_strips):
        lanes = slice(s * CM_STRIP, (s + 1) * CM_STRIP)
        y = (cv_ref[s] - mu) * inv * lw_ref[:, lanes] + lb_ref[:, lanes]
        v_ref[:, lanes] = _silu(y).astype(BF16)
    out_ref[...] = _dot(v_ref[...], w2_ref[...])
    _post_rows(o_ref, x_ref, out_ref, nw_ref, mod_ref, 1.0, inv_ref, rows_ref)


def _cm_out(trunk, u, c, layer, x, mod3, nw):
    tm = 256
    nblk = trunk.tokens // tm
    blocks_per_seq = trunk.seq_len // tm
    hb = tm // CM_HALO
    last = trunk.tokens // CM_HALO - 1
    n_strips = D // CM_STRIP
    span = tm + (CM_HALO - CM_KERNEL // 2 + CM_KERNEL - 1) // SUBLANES * SUBLANES
    row = pl.BlockSpec((None, 1, D), lambda i: (layer, 0, 0))
    return pl.pallas_call(
        functools.partial(_cm_out_kernel, blocks_per_seq=blocks_per_seq),
        grid=(nblk,),
        in_specs=[
            pl.BlockSpec((tm, D), lambda i: (i, 0)),
            pl.BlockSpec((CM_HALO, D), lambda i: (jnp.maximum(i * hb - 1, 0), 0)),
            pl.BlockSpec((CM_HALO, D), lambda i: (jnp.minimum(i * hb + hb, last), 0)),
            pl.BlockSpec((None, n_strips, CM_KERNEL, SUBLANES, CM_STRIP), lambda i: (layer, 0, 0, 0, 0)),
            pl.BlockSpec((None, n_strips, SUBLANES, CM_STRIP), lambda i: (layer, 0, 0, 0)),
            row, row,
            pl.BlockSpec((None, D, D), lambda i: (layer, 0, 0)),
            pl.BlockSpec((tm, D), lambda i: (i, 0)),
            trunk.mod_spec(tm),
            pl.BlockSpec((2, D), lambda i: (0, 0)),
        ],
        out_specs=pl.BlockSpec((tm, D), lambda i: (i, 0)),
        out_shape=jax.ShapeDtypeStruct((trunk.tokens, D), F32),
        scratch_shapes=[pltpu.VMEM((n_strips, tm + 2 * CM_HALO, CM_STRIP), F32),
                        pltpu.VMEM((SUBLANES - 1, span, CM_STRIP), F32),
                        pltpu.VMEM((n_strips, tm, CM_STRIP), F32),
                        pltpu.VMEM((tm, D), BF16),
                        pltpu.VMEM((tm, D), F32),
                        pltpu.VMEM((tm, LANES), F32), pltpu.VMEM((2 * NORM_ROWS, D), F32)],
        compiler_params=_params(("arbitrary",), 48),
    )(u, u, u, c["dw_w"], c["dw_b"], c["ln_w"], c["ln_b"], c["pw2"], x, mod3, nw)


def _strip_rows(v, strip):
    v = v.reshape(v.shape[:-1] + (v.shape[-1] // strip, 1, strip))
    return jnp.broadcast_to(v, v.shape[:-2] + (SUBLANES, strip))


def _run_trunk(trunk, x, mod_t, h0_all, grid_rows, w, ffn_bf16, ffn_f32=None):
    state_buf = None

    def ffn(x, i, half, mod3, nw):
        nxt = (i, 1) if half == 0 else (i + 1, 0)
        cast = None
        if ffn_f32 is not None and nxt[0] < DEPTH and nxt not in ffn_bf16:
            cast = (*ffn_f32, *nxt)
        x, made = _ffn(trunk, x, mod3, nw, ffn_bf16[(i, half)], cast)
        if made is not None:
            ffn_bf16[nxt] = made
        return x

    for i in range(DEPTH):
        kind, j = i % 3, i // 3
        x = ffn(x, i, 0, mod_t[i, 0:3], w["norm_w"][i, 0])
        mod3, nw = mod_t[i, 3:6], w["norm_w"][i, 1]
        if kind == 0:
            x = _pool_mixer(trunk, x, mod3, nw, w["pool_w"], w["pool_scale"], j, grid_rows)
        elif kind == 1:
            h0 = None if h0_all is None else h0_all[:, j].reshape(trunk.nseq, 2, SSD_D_INNER, SSD_STATE)
            x, state_buf = _ssd_mixer(trunk, x, mod3, nw, h0, w["ssd"], j, state_buf)
        else:
            u = _cm_in(trunk, x, mod3, nw, w["cm"]["pw1"], j)
            x = _cm_out(trunk, u, w["cm"], j, x, mod3, nw)
        x = ffn(x, i, 1, mod_t[i, 6:9], w["norm_w"][i, 2])
    return x, state_buf


def kernel(x_prompt, x_sample, state_ssd, c, c_ctx, norm_w, ada_w, ada_b, ffn_wg, ffn_wu, ffn_wd, pool_w, pool_scale, ssd_in_w, ssd_conv_w, ssd_conv_b, ssd_a_log, ssd_dt_bias, ssd_d, ssd_norm_w, ssd_out_w, cm_pw1, cm_dw_w, cm_dw_b, cm_ln_w, cm_ln_b, cm_pw2):
    n_ctx, l_ctx, _ = x_prompt.shape
    n_lat, l_lat, _ = x_sample.shape
    ctx = _Trunk(n_ctx, l_ctx, shared_cond=True)
    lat = _Trunk(n_lat, l_lat, shared_cond=False)

    cond8 = jnp.zeros((8, D), F32).at[0].set(c_ctx).at[1:1 + n_lat].set(c)
    mod = _ada_mod(cond8, ada_w, ada_b).reshape(DEPTH, 8, N_MOD, D)
    mod = jnp.transpose(mod, (0, 2, 1, 3))[:, :, :, None, :]
    mod_ctx, mod_lat = mod[:, :, 0:1], mod[:, :, 1:1 + n_lat]

    n_ssd = ssd_in_w.shape[0]
    n_cm = cm_pw1.shape[0]
    ndt = 2 * SSD_HEADS
    w = {
        "norm_w": norm_w,
        "pool_w": pool_w.astype(BF16),
        "pool_scale": pool_scale[:, None, :],
        "ssd": dict(
            in_w=ssd_in_w.astype(BF16),
            conv_w=_strip_rows(ssd_conv_w, SSD_STRIP).transpose(0, 2, 1, 3, 4),
            conv_b=_strip_rows(ssd_conv_b, SSD_STRIP),
            dt_bias=ssd_dt_bias.reshape(n_ssd, 1, ndt), a_log=ssd_a_log.reshape(n_ssd, 1, ndt),
            d_skip=jnp.repeat(ssd_d, SSD_HEAD_DIM, axis=-1).reshape(n_ssd, SSD_GROUPS, 1, SSD_GROUP_DIM),
            gnorm=ssd_norm_w[:, None, :], out_w=ssd_out_w.astype(BF16)),
        "cm": dict(
            pw1=cm_pw1.astype(BF16),
            dw_w=_strip_rows(cm_dw_w, CM_STRIP).transpose(0, 2, 1, 3, 4),
            dw_b=_strip_rows(cm_dw_b, CM_STRIP),
            ln_w=cm_ln_w[:, None, :], ln_b=cm_ln_b[:, None, :], pw2=cm_pw2.astype(BF16)),
    }

    ffn_bf16 = {(0, 0): (ffn_wg[0, 0].astype(BF16), ffn_wu[0, 0].astype(BF16), ffn_wd[0, 0].astype(BF16))}
    y_ctx, new_state = _run_trunk(ctx, x_prompt.reshape(ctx.tokens, D), mod_ctx, None, None, w,
                                  ffn_bf16, (ffn_wg, ffn_wu, ffn_wd))
    y_lat, _ = _run_trunk(lat, x_sample.reshape(lat.tokens, D), mod_lat, state_ssd,
                          l_lat // GRID_W, w, ffn_bf16)

    new_state = new_state.reshape(n_ctx, n_ssd, 2, SSD_HEADS, SSD_HEAD_DIM, SSD_STATE)
    return (y_ctx.reshape(x_prompt.shape), y_lat.reshape(x_sample.shape),
            new_state.astype(x_prompt.dtype))
```
